```python
import numpy as np
import jax, jax.numpy as jnp
from jax import lax

D_MODEL = 2048
BATCH = 16
SEQ = 2048
DEPTH = 4

CHUNK = 64
PLE_DIM = 256
LRU_WIDTH = 1024
LRU_GROUPS = 16
LRU_CONV = 4
LRU_C = 8.0
GLA_HEADS = 4
GLA_DK = 128
GLA_DV = 256
GLA_RANK = 16
GLA_GATE_NORM = 16.0
GDN_HEADS = 8
GDN_DK = 128
GDN_DV = 128
GDN_CONV = 4
N_BRANCH = 3
D_FF = 5632
FFN_CONV = 3
LN_EPS = 1e-5
NORM_EPS = 1e-6
DEEPNORM_ALPHA = (2 * DEPTH) ** 0.25
DEEPNORM_BETA = (8 * DEPTH) ** -0.25
IN_SPLITS = (LRU_WIDTH, LRU_WIDTH,
             GLA_HEADS * GLA_DK, GLA_HEADS * GLA_DK, GLA_HEADS * GLA_DV, GLA_RANK, GLA_HEADS * GLA_DV,
             GDN_HEADS * GDN_DK, GDN_HEADS * GDN_DK, GDN_HEADS * GDN_DV, GDN_HEADS, GDN_HEADS, GDN_HEADS * GDN_DV)
D_IN = sum(IN_SPLITS)
GDN_QKV = 2 * GDN_HEADS * GDN_DK + GDN_HEADS * GDN_DV

kernel_name = 'hybrid_rglru_gla_gdn_convffn_deepnorm'


def layer_norm(x, g, b):
    xf = x.astype(jnp.float32)
    mu = xf.mean(-1, keepdims=True)
    var = jnp.square(xf - mu).mean(-1, keepdims=True)
    return ((xf - mu) * lax.rsqrt(var + LN_EPS) * g.astype(jnp.float32) + b.astype(jnp.float32)).astype(x.dtype)


def head_rms_norm(x, g):
    return x * lax.rsqrt(jnp.mean(jnp.square(x), -1, keepdims=True) + NORM_EPS) * g.astype(jnp.float32)


def l2_normalize(x):
    return x * lax.rsqrt(jnp.sum(jnp.square(x), -1, keepdims=True) + NORM_EPS)


def causal_depthwise_conv(x, w):
    width = w.shape[0]
    return lax.conv_general_dilated(
        x, w[:, None, :].astype(x.dtype), window_strides=(1,), padding=[(width - 1, 0)],
        dimension_numbers=('NWC', 'WIO', 'NWC'), feature_group_count=x.shape[-1])


def rg_lru(x, w_r, b_r, w_i, b_i, lam):
    B, S, W = x.shape
    xg = x.reshape(B, S, LRU_GROUPS, W // LRU_GROUPS)
    r = jax.nn.sigmoid(jnp.einsum('bsgi,gij->bsgj', xg, w_r.astype(jnp.float32)).reshape(B, S, W) + b_r.astype(jnp.float32))
    i = jax.nn.sigmoid(jnp.einsum('bsgi,gij->bsgj', xg, w_i.astype(jnp.float32)).reshape(B, S, W) + b_i.astype(jnp.float32))
    log_a = -LRU_C * r * jax.nn.softplus(-lam.astype(jnp.float32))
    a = jnp.exp(log_a)
    u = jnp.sqrt(-jnp.expm1(2.0 * log_a)) * (i * x)

    def combine(lhs, rhs):
        a1, b1 = lhs
        a2, b2 = rhs
        return a1 * a2, a2 * b1 + b2

    _, h = lax.associative_scan(combine, (a, u), axis=1)
    return h


def gla_chunked(q, k, v, g):
    B, S, H, K = q.shape
    V = v.shape[-1]
    N = S // CHUNK
    ch = lambda t: t.reshape(B, N, CHUNK, H, -1)
    q = ch(q) * K ** -0.5
    k = ch(k)
    v = ch(v)
    b = jnp.cumsum(ch(g), axis=2)
    b_last = b[:, :, -1]
    q_dec = q * jnp.exp(b)
    k_dec = k * jnp.exp(b_last[:, :, None] - b)
    scores = jnp.einsum('bnthk,bnshk->bnhts', q_dec, k * jnp.exp(-b))
    causal = jnp.tril(jnp.ones((CHUNK, CHUNK), bool))
    o_intra = jnp.einsum('bnhts,bnshv->bnthv', jnp.where(causal, scores, 0.0), v)

    def step(state, inp):
        qd, kd, vc, dl = inp
        o = jnp.einsum('bthk,bhkv->bthv', qd, state)
        state = state * dl[..., None] + jnp.einsum('bshk,bshv->bhkv', kd, vc)
        return state, o

    xs = tuple(jnp.moveaxis(t, 1, 0) for t in (q_dec, k_dec, v, jnp.exp(b_last)))
    _, o_inter = lax.scan(step, jnp.zeros((B, H, K, V), jnp.float32), xs)
    return (o_intra + jnp.moveaxis(o_inter, 0, 1)).reshape(B, S, H, V)


def gated_delta_chunked(q, k, v, g, beta):
    B, S, H, K = q.shape
    V = v.shape[-1]
    N = S // CHUNK
    to_chunks = lambda t: jnp.swapaxes(t.reshape(B, N, CHUNK, H, -1), 2, 3)
    q = to_chunks(q) * K ** -0.5
    k = to_chunks(k)
    v = to_chunks(v)
    gam = jnp.cumsum(to_chunks(g[..., None])[..., 0], axis=-1)
    bt = to_chunks(beta[..., None])
    incl = jnp.tril(jnp.ones((CHUNK, CHUNK), bool))
    strict = jnp.tril(jnp.ones((CHUNK, CHUNK), bool), -1)
    diff = gam[..., :, None] - gam[..., None, :]
    decay_mat = jnp.where(incl, jnp.exp(jnp.where(incl, diff, 0.0)), 0.0)
    kk = jnp.einsum('bnhtk,bnhsk->bnhts', k * bt, k) * decay_mat
    tri = jnp.where(strict, kk, 0.0) + jnp.eye(CHUNK, dtype=kk.dtype)
    rhs = jnp.concatenate([v * bt, k * bt * jnp.exp(gam)[..., None]], axis=-1)
    sol = lax.linalg.triangular_solve(tri, rhs, left_side=True, lower=True, unit_diagonal=True)
    u, w = sol[..., :V], sol[..., V:]
    qk = jnp.einsum('bnhtk,bnhsk->bnhts', q, k) * decay_mat
    q_dec = q * jnp.exp(gam)[..., None]
    k_dec = k * jnp.exp(gam[..., -1:] - gam)[..., None]
    last = jnp.exp(gam[..., -1])

    def step(state, inp):
        qd, kd, uc, wc, a, dl = inp
        v_new = uc - jnp.einsum('bhtk,bhkv->bhtv', wc, state)
        o = jnp.einsum('bhtk,bhkv->bhtv', qd, state) + jnp.einsum('bhts,bhsv->bhtv', a, v_new)
        state = state * dl[..., None, None] + jnp.einsum('bhsk,bhsv->bhkv', kd, v_new)
        return state, o

    xs = tuple(jnp.moveaxis(t, 1, 0) for t in (q_dec, k_dec, u, w, qk, last))
    _, o = lax.scan(step, jnp.zeros((B, H, K, V), jnp.float32), xs)
    return jnp.swapaxes(jnp.moveaxis(o, 0, 1), 2, 3).reshape(B, S, H, V)


def setup_inputs(seed: int = 0) -> dict:
    key = jax.random.key(seed)
    ks = jax.random.split(key, 40)
    f32 = jnp.float32
    nrm = lambda k, shape, scale: jax.random.normal(k, shape, f32) * scale
    gain = lambda k, shape: 1.0 + 0.02 * jax.random.normal(k, shape, f32)
    a0 = jax.random.uniform(ks[9], (DEPTH, LRU_WIDTH), f32, 0.9, 0.999) ** (1.0 / LRU_C)
    dt = jnp.exp(jax.random.uniform(ks[14], (DEPTH, GDN_HEADS), f32, np.log(1e-3), np.log(1e-1)))
    gs = LRU_WIDTH // LRU_GROUPS
    return {
        'x': nrm(ks[0], (BATCH, SEQ, D_MODEL), 1.0),
        'p': nrm(ks[1], (DEPTH, BATCH, SEQ, PLE_DIM), 1.0),
        'w_in': nrm(ks[2], (DEPTH, D_MODEL, D_IN), D_MODEL ** -0.5),
        'lru_conv_w': nrm(ks[3], (DEPTH, LRU_CONV, LRU_WIDTH), LRU_CONV ** -0.5),
        'lru_conv_b': nrm(ks[4], (DEPTH, LRU_WIDTH), 0.02),
        'lru_wr': nrm(ks[5], (DEPTH, LRU_GROUPS, gs, gs), gs ** -0.5),
        'lru_br': nrm(ks[6], (DEPTH, LRU_WIDTH), 0.02),
        'lru_wi': nrm(ks[7], (DEPTH, LRU_GROUPS, gs, gs), gs ** -0.5),
        'lru_bi': nrm(ks[8], (DEPTH, LRU_WIDTH), 0.02),
        'lru_lambda': jnp.log(a0) - jnp.log1p(-a0),
        'gla_wg2': nrm(ks[10], (DEPTH, GLA_RANK, GLA_HEADS * GLA_DK), GLA_RANK ** -0.5),
        'gla_bg2': nrm(ks[11], (DEPTH, GLA_HEADS * GLA_DK), 0.1),
        'gla_norm_g': gain(ks[12], (DEPTH, GLA_DV)),
        'gdn_conv_w': nrm(ks[13], (DEPTH, GDN_CONV, GDN_QKV), GDN_CONV ** -0.5),
        'gdn_a_log': jnp.log(jax.random.uniform(ks[15], (DEPTH, GDN_HEADS), f32, 1.0, 16.0)),
        'gdn_dt_bias': dt + jnp.log(-jnp.expm1(-dt)),
        'gdn_norm_g': gain(ks[16], (DEPTH, GDN_DV)),
        'w_branch': nrm(ks[17], (DEPTH, N_BRANCH, LRU_WIDTH, D_MODEL), LRU_WIDTH ** -0.5),
        'w_merge': nrm(ks[18], (DEPTH, N_BRANCH, D_MODEL, D_MODEL), D_MODEL ** -0.5),
        'b_merge': nrm(ks[19], (DEPTH, N_BRANCH, D_MODEL), 0.02),
        'w_out': nrm(ks[20], (DEPTH, D_MODEL, D_MODEL), D_MODEL ** -0.5 * DEEPNORM_BETA),
        'ln1_g': gain(ks[21], (DEPTH, D_MODEL)),
        'ln1_b': nrm(ks[22], (DEPTH, D_MODEL), 0.02),
        'ffn_w_up': nrm(ks[23], (DEPTH, D_MODEL, D_FF), D_MODEL ** -0.5),
        'ffn_w_gate': nrm(ks[24], (DEPTH, D_MODEL, D_FF), D_MODEL ** -0.5),
        'ffn_conv_w': nrm(ks[25], (DEPTH, FFN_CONV, D_FF), FFN_CONV ** -0.5),
        'ffn_conv_b': nrm(ks[26], (DEPTH, D_FF), 0.02),
        'ffn_w_down': nrm(ks[27], (DEPTH, D_FF, D_MODEL), D_FF ** -0.5 * DEEPNORM_BETA),
        'ple_w_proj': nrm(ks[28], (DEPTH, PLE_DIM, D_MODEL), PLE_DIM ** -0.5 * DEEPNORM_BETA),
        'ple_w_gate': nrm(ks[29], (DEPTH, D_MODEL, D_MODEL), D_MODEL ** -0.5),
        'ln2_g': gain(ks[30], (DEPTH, D_MODEL)),
        'ln2_b': nrm(ks[31], (DEPTH, D_MODEL), 0.02),
    }


def reference(x, p, w_in, lru_conv_w, lru_conv_b, lru_wr, lru_br, lru_wi, lru_bi, lru_lambda,
              gla_wg2, gla_bg2, gla_norm_g, gdn_conv_w, gdn_a_log, gdn_dt_bias, gdn_norm_g,
              w_branch, w_merge, b_merge, w_out, ln1_g, ln1_b,
              ffn_w_up, ffn_w_gate, ffn_conv_w, ffn_conv_b, ffn_w_down,
              ple_w_proj, ple_w_gate, ln2_g, ln2_b):
    B, S, _ = x.shape
    f32 = jnp.float32
    split_at = np.cumsum(IN_SPLITS)[:-1].tolist()
    qk_split = [GDN_HEADS * GDN_DK, 2 * GDN_HEADS * GDN_DK]
    for i in range(DEPTH):
        h = x
        (lru_x, lru_gate, gq, gk, gv, glr, gog,
         dq, dk, dv, da, db, dog) = jnp.split(h @ w_in[i], split_at, axis=-1)

        xa = (causal_depthwise_conv(lru_x, lru_conv_w[i]) + lru_conv_b[i]).astype(f32)
        ya = rg_lru(xa, lru_wr[i], lru_br[i], lru_wi[i], lru_bi[i], lru_lambda[i]) * jax.nn.gelu(lru_gate.astype(f32))

        fg = jax.nn.log_sigmoid((glr @ gla_wg2[i] + gla_bg2[i]).astype(f32)) / GLA_GATE_NORM
        ob = gla_chunked(gq.astype(f32).reshape(B, S, GLA_HEADS, GLA_DK),
                         gk.astype(f32).reshape(B, S, GLA_HEADS, GLA_DK),
                         gv.astype(f32).reshape(B, S, GLA_HEADS, GLA_DV),
                         fg.reshape(B, S, GLA_HEADS, GLA_DK))
        yb = (head_rms_norm(ob, gla_norm_g[i])
              * jax.nn.silu(gog.astype(f32).reshape(B, S, GLA_HEADS, GLA_DV))).reshape(B, S, -1)

        qkv = jax.nn.silu(causal_depthwise_conv(jnp.concatenate([dq, dk, dv], -1), gdn_conv_w[i]).astype(f32))
        cq, ck, cv = jnp.split(qkv, qk_split, axis=-1)
        beta = jax.nn.sigmoid(db.astype(f32))
        decay = -jnp.exp(gdn_a_log[i].astype(f32)) * jax.nn.softplus(da.astype(f32) + gdn_dt_bias[i].astype(f32))
        oc = gated_delta_chunked(l2_normalize(cq.reshape(B, S, GDN_HEADS, GDN_DK)),
                                 l2_normalize(ck.reshape(B, S, GDN_HEADS, GDN_DK)),
                                 cv.reshape(B, S, GDN_HEADS, GDN_DV), decay, beta)
        yc = (head_rms_norm(oc, gdn_norm_g[i])
              * jax.nn.silu(dog.astype(f32).reshape(B, S, GDN_HEADS, GDN_DV))).reshape(B, S, -1)

        branches = (ya, yb, yc)
        merged = jnp.zeros((B, S, D_MODEL), f32)
        for n in range(N_BRANCH):
            gate = jax.nn.sigmoid((h @ w_merge[i, n] + b_merge[i, n]).astype(f32))
            merged = merged + gate * (branches[n].astype(x.dtype) @ w_branch[i, n]).astype(f32)
        mix = merged.astype(x.dtype) @ w_out[i]
        x = layer_norm(DEEPNORM_ALPHA * x + mix, ln1_g[i], ln1_b[i])

        up = x @ ffn_w_up[i]
        gt = causal_depthwise_conv(x @ ffn_w_gate[i], ffn_conv_w[i]) + ffn_conv_b[i]
        ffn = (jax.nn.silu(gt) * up) @ ffn_w_down[i]
        ple = (p[i] @ ple_w_proj[i]) * jax.nn.sigmoid(x @ ple_w_gate[i])
        x = layer_norm(DEEPNORM_ALPHA * x + ffn + ple, ln2_g[i], ln2_b[i])
    return x
```

```python
import functools

import jax
import jax.numpy as jnp
from jax import lax
from jax.experimental import pallas as pl
from jax.experimental.pallas import tpu as pltpu

F32 = jnp.float32
BF16 = jnp.bfloat16

D_MODEL = 2048
DEPTH = 4
CHUNK = 64
PLE_DIM = 256
LRU_WIDTH = 1024
LRU_GROUPS = 16
LRU_CONV = 4
LRU_C = 8.0
GLA_HEADS = 4
GLA_DK = 128
GLA_DV = 256
GLA_RANK = 16
GLA_GATE_NORM = 16.0
GDN_HEADS = 8
GDN_DK = 128
GDN_DV = 128
GDN_CONV = 4
N_BRANCH = 3
D_FF = 5632
FFN_CONV = 3
LN_EPS = 1e-5
NORM_EPS = 1e-6
DEEPNORM_ALPHA = (2 * DEPTH) ** 0.25

MAIN_COLS = 9216
SMALL_COLS = 128
BLK_LRU_X, BLK_LRU_GATE, BLK_GQK, BLK_GV, BLK_GOG, BLK_DOG = 3, 4, 5, 6, 7, 8
DA_LANE = 16
DB_LANE = 24
LRU_BD = 256
TAIL = 8

VMEM_LIMIT = 56 * 1024 * 1024


def _dot(a, b):
    return jnp.dot(a, b, preferred_element_type=F32)


def _dot_nt(a, b):
    return lax.dot_general(a, b, (((1,), (1,)), ((), ())), preferred_element_type=F32)


def _dot_tn(a, b):
    return lax.dot_general(a, b, (((0,), (0,)), ((), ())), preferred_element_type=F32)


def _split2(x):
    hi = x.astype(BF16)
    lo = (x - hi.astype(F32)).astype(BF16)
    return hi, lo


def _dot_hp(a, b):
    ah, al = _split2(a)
    bh, bl = _split2(b)
    return _dot(ah, bh) + (_dot(ah, bl) + _dot(al, bh))


def _split3(x):
    x1 = x.astype(BF16)
    r = x - x1.astype(F32)
    x2 = r.astype(BF16)
    x3 = (r - x2.astype(F32)).astype(BF16)
    return x1, x2, x3


def _softplus(x):
    return jnp.maximum(x, 0.0) + jnp.log1p(jnp.exp(-jnp.abs(x)))


def _sigmoid(x):
    return 1.0 / (1.0 + jnp.exp(-x))


def _silu(x):
    return x * _sigmoid(x)


def _layer_norm(y, g, b):
    mu = jnp.mean(y, axis=-1, keepdims=True)
    d = y - mu
    var = jnp.mean(d * d, axis=-1, keepdims=True)
    return d * lax.rsqrt(var + LN_EPS) * g + b


def _tile(dim, pref):
    t = min(dim, pref)
    assert dim % t == 0, (dim, pref)
    return t


def _params(*sem):
    return pltpu.CompilerParams(dimension_semantics=sem, vmem_limit_bytes=VMEM_LIMIT)


def _mm_body(x_ref, w_ref, o_ref):
    o_ref[...] = _dot(x_ref[...], w_ref[...]).astype(o_ref.dtype)


def _matmul(x, w, out_dtype, tm, tn):
    M, K = x.shape
    N = w.shape[1]
    tm, tn = _tile(M, tm), _tile(N, tn)
    return pl.pallas_call(
        _mm_body,
        grid=(N // tn, M // tm),
        in_specs=[pl.BlockSpec((tm, K), lambda j, i: (i, 0)),
                  pl.BlockSpec((K, tn), lambda j, i: (0, j))],
        out_specs=pl.BlockSpec((tm, tn), lambda j, i: (i, j)),
        out_shape=jax.ShapeDtypeStruct((M, N), out_dtype),
        compiler_params=_params("parallel", "parallel"),
    )(x, w)


def _lru_body(x_ref, gate_ref, cw_ref, cb_ref, wbd_ref, bri_ref, lam_ref, o_ref, xbuf, hcarry):
    tc = x_ref.shape[1]

    @pl.when(pl.program_id(1) == 0)
    def _():
        xbuf[0:TAIL, :] = jnp.zeros((TAIL, LRU_WIDTH), F32)
        hcarry[...] = jnp.zeros_like(hcarry)

    xbuf[TAIL:TAIL + tc, :] = x_ref[0]
    cw = cw_ref[...]
    xa = cb_ref[...] + cw[3:4, :] * x_ref[0]
    for j in range(LRU_CONV - 1):
        xa = xa + cw[j:j + 1, :] * xbuf[pl.ds(TAIL - (LRU_CONV - 1) + j, tc), :]
    xbuf[0:TAIL, :] = xbuf[tc:tc + TAIL, :]

    lam = lam_ref[...]
    neg_c_sp = -LRU_C * _softplus(-lam)
    row = lax.broadcasted_iota(jnp.int32, (tc, LRU_BD), 0)
    for k in range(LRU_WIDTH // LRU_BD):
        cs = slice(k * LRU_BD, (k + 1) * LRU_BD)
        xk = xa[:, cs]
        ri = _dot(xk.astype(BF16), wbd_ref[k]) + bri_ref[k]
        r = _sigmoid(ri[:, :LRU_BD])
        ig = _sigmoid(ri[:, LRU_BD:])
        log_a = r * neg_c_sp[:, cs]
        a = jnp.exp(log_a)
        u = jnp.sqrt(-jnp.tanh(log_a) * (a * a + 1.0)) * (ig * xk)
        d = 1
        while d < tc:
            keep = row >= d
            a_s = jnp.where(keep, pltpu.roll(a, d, 0), 1.0)
            u_s = jnp.where(keep, pltpu.roll(u, d, 0), 0.0)
            u = a * u_s + u
            a = a * a_s
            d *= 2
        h = u + a * hcarry[:, cs]
        hcarry[:, cs] = h[tc - 1:tc, :]
        o_ref[0, :, cs] = (h * jax.nn.gelu(gate_ref[0, :, cs], approximate=True)).astype(o_ref.dtype)


def _lru(proj, cw, cb, wbd, bri, lam, tc):
    B, S, _ = proj.shape
    tc = _tile(S, tc)
    W = LRU_WIDTH
    nk = W // LRU_BD
    full = lambda *shape: pl.BlockSpec(shape, lambda b, s: (0,) * len(shape))
    return pl.pallas_call(
        _lru_body,
        grid=(B, S // tc),
        in_specs=[pl.BlockSpec((1, tc, W), lambda b, s: (b, s, BLK_LRU_X)),
                  pl.BlockSpec((1, tc, W), lambda b, s: (b, s, BLK_LRU_GATE)),
                  full(LRU_CONV, W), full(1, W), full(nk, LRU_BD, 2 * LRU_BD), full(nk, 1, 2 * LRU_BD), full(1, W)],
        out_specs=pl.BlockSpec((1, tc, W), lambda b, s: (b, s, 0)),
        out_shape=jax.ShapeDtypeStruct((B, S, W), BF16),
        scratch_shapes=[pltpu.VMEM((tc + TAIL, W), F32), pltpu.VMEM((1, W), F32)],
        compiler_params=_params("parallel", "arbitrary"),
    )(proj, proj, cw, cb, wbd, bri, lam)


def _gla_body(qk_ref, v_ref, og_ref, sm_ref, wg2_ref, bg2_ref, ng_ref, o_ref, state):
    H, K, V, C = GLA_HEADS, GLA_DK, GLA_DV, CHUNK

    @pl.when(pl.program_id(1) == 0)
    def _():
        state[...] = jnp.zeros_like(state)

    z = _dot(sm_ref[0].astype(BF16), wg2_ref[...]) + bg2_ref[...]
    fg = (jnp.minimum(z, 0.0) - jnp.log1p(jnp.exp(-jnp.abs(z)))) * (1.0 / GLA_GATE_NORM)
    ti = lax.broadcasted_iota(jnp.int32, (C, C), 0)
    si = lax.broadcasted_iota(jnp.int32, (C, C), 1)
    causal = ti >= si
    tri = jnp.where(causal, 1.0, 0.0).astype(BF16)
    f1, f2, f3 = _split3(fg)
    bcum = _dot(tri, f1) + (_dot(tri, f2) + _dot(tri, f3))
    ng = ng_ref[...]
    for h in range(H):
        q = qk_ref[0, :, h * K:(h + 1) * K] * (K ** -0.5)
        k = qk_ref[0, :, (H + h) * K:(H + h + 1) * K]
        v = v_ref[0, :, h * V:(h + 1) * V].astype(BF16)
        b = bcum[:, h * K:(h + 1) * K]
        bl = b[C - 1:C, :]
        q_dec = (q * jnp.exp(b)).astype(BF16)
        k_neg = (k * jnp.exp(-b)).astype(BF16)
        k_dec = (k * jnp.exp(bl - b)).astype(BF16)
        scores = jnp.where(causal, _dot_nt(q_dec, k_neg), 0.0)
        st = state[h]
        o = _dot(scores.astype(BF16), v) + _dot_nt(q_dec, st.astype(BF16))
        state[h] = st * jnp.exp(bl) + _dot_tn(v, k_dec)
        o = o * lax.rsqrt(jnp.mean(o * o, axis=-1, keepdims=True) + NORM_EPS) * ng
        o_ref[0, :, h * V:(h + 1) * V] = (o * _silu(og_ref[0, :, h * V:(h + 1) * V])).astype(o_ref.dtype)


def _gla(proj, small, wg2p, bg2, ng):
    B, S, _ = proj.shape
    C = CHUNK
    W = GLA_HEADS * GLA_DV
    full = lambda *shape: pl.BlockSpec(shape, lambda b, s: (0,) * len(shape))
    return pl.pallas_call(
        _gla_body,
        grid=(B, S // C),
        in_specs=[pl.BlockSpec((1, C, W), lambda b, s: (b, s, BLK_GQK)),
                  pl.BlockSpec((1, C, W), lambda b, s: (b, s, BLK_GV)),
                  pl.BlockSpec((1, C, W), lambda b, s: (b, s, BLK_GOG)),
                  pl.BlockSpec((1, C, SMALL_COLS), lambda b, s: (b, s, 0)),
                  full(SMALL_COLS, GLA_HEADS * GLA_DK), full(1, GLA_HEADS * GLA_DK), full(1, GLA_DV)],
        out_specs=pl.BlockSpec((1, C, W), lambda b, s: (b, s, 0)),
        out_shape=jax.ShapeDtypeStruct((B, S, W), BF16),
        scratch_shapes=[pltpu.VMEM((GLA_HEADS, GLA_DV, GLA_DK), F32)],
        compiler_params=_params("parallel", "arbitrary"),
    )(proj, proj, proj, small, wg2p, bg2, ng)


def _unit_lower_inverse(n_strict, ti, si):
    C = n_strict.shape[0]
    eye = jnp.where(ti == si, 1.0, 0.0)
    t = eye - jnp.where((ti // 2 == si // 2), n_strict, 0.0)
    d = 2
    while d < C:
        c_d = jnp.where((ti // (2 * d) == si // (2 * d)) & (ti % (2 * d) >= d) & (si % (2 * d) < d), n_strict, 0.0)
        t = t - _dot_hp(_dot_hp(t, c_d), t)
        d *= 2
    return t


def _gdn_body(qkv_ref, og_ref, sm_ref, cw_ref, alog_ref, dtb_ref, ng_ref, o_ref, xbuf, state):
    H, K, V, C = GDN_HEADS, GDN_DK, GDN_DV, CHUNK
    QKV = qkv_ref.shape[2]

    @pl.when(pl.program_id(1) == 0)
    def _():
        xbuf[0:TAIL, :] = jnp.zeros((TAIL, QKV), F32)
        state[...] = jnp.zeros_like(state)

    xbuf[TAIL:TAIL + C, :] = qkv_ref[0]
    cw = cw_ref[...]
    conv = cw[3:4, :] * qkv_ref[0]
    for j in range(GDN_CONV - 1):
        conv = conv + cw[j:j + 1, :] * xbuf[pl.ds(TAIL - (GDN_CONV - 1) + j, C), :]
    xbuf[0:TAIL, :] = xbuf[C:C + TAIL, :]
    qkv = _silu(conv)

    sm = sm_ref[0]
    beta_all = _sigmoid(sm)
    g_all = -jnp.exp(alog_ref[...]) * _softplus(sm + dtb_ref[...])
    ti = lax.broadcasted_iota(jnp.int32, (C, C), 0)
    si = lax.broadcasted_iota(jnp.int32, (C, C), 1)
    incl = ti >= si
    strict = ti > si
    tri = jnp.where(incl, 1.0, 0.0).astype(BF16)
    g1, g2, g3 = _split3(g_all)
    gam = _dot(tri, g1) + (_dot(tri, g2) + _dot(tri, g3))
    sel = jnp.where(lax.broadcasted_iota(jnp.int32, (TAIL, SMALL_COLS), 1)
                    == lax.broadcasted_iota(jnp.int32, (TAIL, SMALL_COLS), 0) + DA_LANE, 1.0, 0.0).astype(BF16)
    m1, m2, m3 = _split3(gam)
    gam_rows = _dot_nt(sel, m1) + (_dot_nt(sel, m2) + _dot_nt(sel, m3))
    ng = ng_ref[...]
    for h in range(H):
        gcol = gam[:, DA_LANE + h:DA_LANE + h + 1]
        grow = gam_rows[h:h + 1, :]
        bt = beta_all[:, DB_LANE + h:DB_LANE + h + 1]
        decay = jnp.where(incl, jnp.exp(jnp.where(incl, gcol - grow, 0.0)), 0.0)
        q = qkv[:, h * K:(h + 1) * K]
        k = qkv[:, (H + h) * K:(H + h + 1) * K]
        v = qkv[:, (2 * H) * K + h * V:(2 * H) * K + (h + 1) * V]
        q = q * lax.rsqrt(jnp.sum(q * q, axis=-1, keepdims=True) + NORM_EPS) * (K ** -0.5)
        k = k * lax.rsqrt(jnp.sum(k * k, axis=-1, keepdims=True) + NORM_EPS)
        eg = jnp.exp(gcol)
        kb = k * bt
        k16 = k.astype(BF16)
        kk = _dot_nt(kb.astype(BF16), k16) * decay
        t_inv = _unit_lower_inverse(jnp.where(strict, kk, 0.0), ti, si)
        u = _dot_hp(t_inv, v * bt)
        w = _dot_hp(t_inv, kb * eg)
        qk = _dot_nt(q.astype(BF16), k16) * decay
        gl = gcol[C - 1:C, :]
        q_dec = (q * eg).astype(BF16)
        k_dec = (k * jnp.exp(gl - gcol)).astype(BF16)
        st = state[h]
        st16 = st.astype(BF16)
        v_new = u - _dot(w.astype(BF16), st16)
        vn16 = v_new.astype(BF16)
        o = _dot(q_dec, st16) + _dot(qk.astype(BF16), vn16)
        state[h] = st * jnp.exp(gl) + _dot_tn(k_dec, vn16)
        o = o * lax.rsqrt(jnp.mean(o * o, axis=-1, keepdims=True) + NORM_EPS) * ng
        o_ref[0, :, h * V:(h + 1) * V] = (o * _silu(og_ref[0, :, h * V:(h + 1) * V])).astype(o_ref.dtype)


def _gdn(proj, small, cw, alog_p, dtb_p, ng):
    B, S, _ = proj.shape
    C = CHUNK
    QKV = 2 * GDN_HEADS * GDN_DK + GDN_HEADS * GDN_DV
    W = GDN_HEADS * GDN_DV
    full = lambda *shape: pl.BlockSpec(shape, lambda b, s: (0,) * len(shape))
    return pl.pallas_call(
        _gdn_body,
        grid=(B, S // C),
        in_specs=[pl.BlockSpec((1, C, QKV), lambda b, s: (b, s, 0)),
                  pl.BlockSpec((1, C, W), lambda b, s: (b, s, BLK_DOG)),
                  pl.BlockSpec((1, C, SMALL_COLS), lambda b, s: (b, s, 0)),
                  full(GDN_CONV, QKV), full(1, SMALL_COLS), full(1, SMALL_COLS), full(1, GDN_DV)],
        out_specs=pl.BlockSpec((1, C, W), lambda b, s: (b, s, 0)),
        out_shape=jax.ShapeDtypeStruct((B, S, W), BF16),
        scratch_shapes=[pltpu.VMEM((C + TAIL, QKV), F32), pltpu.VMEM((GDN_HEADS, GDN_DK, GDN_DV), F32)],
        compiler_params=_params("parallel", "arbitrary"),
    )(proj, proj, small, cw, alog_p, dtb_p, ng)


def _merge_body(h_ref, ya_ref, yb_ref, yc_ref, wm_ref, wb_ref, bm_ref, o_ref):
    hx = h_ref[...]
    acc = None
    for n, y_ref in enumerate((ya_ref, yb_ref, yc_ref)):
        gate = _sigmoid(_dot(hx, wm_ref[n]) + bm_ref[n])
        term = gate * _dot(y_ref[...], wb_ref[n])
        acc = term if acc is None else acc + term
    o_ref[...] = acc.astype(o_ref.dtype)


def _merge(hb, ya, yb, yc, wm, wb, bm, tm, tn):
    M, D = hb.shape
    Wb = ya.shape[1]
    tm, tn = _tile(M, tm), _tile(D, tn)
    ybs = pl.BlockSpec((tm, Wb), lambda j, i: (i, 0))
    return pl.pallas_call(
        _merge_body,
        grid=(D // tn, M // tm),
        in_specs=[pl.BlockSpec((tm, D), lambda j, i: (i, 0)), ybs, ybs, ybs,
                  pl.BlockSpec((N_BRANCH, D, tn), lambda j, i: (0, 0, j)),
                  pl.BlockSpec((N_BRANCH, Wb, tn), lambda j, i: (0, 0, j)),
                  pl.BlockSpec((N_BRANCH, 1, tn), lambda j, i: (0, 0, j))],
        out_specs=pl.BlockSpec((tm, tn), lambda j, i: (i, j)),
        out_shape=jax.ShapeDtypeStruct((M, D), BF16),
        compiler_params=_params("parallel", "parallel"),
    )(hb, ya, yb, yc, wm, wb, bm)


def _out_ln_body(m_ref, w_ref, x_ref, g_ref, b_ref, o_ref, ob_ref):
    y = DEEPNORM_ALPHA * x_ref[...] + _dot(m_ref[...], w_ref[...])
    xn = _layer_norm(y, g_ref[...], b_ref[...])
    o_ref[...] = xn
    ob_ref[...] = xn.astype(BF16)


def _out_ln(merged, w, x, g, b, tm):
    M, D = x.shape
    tm = _tile(M, tm)
    row = pl.BlockSpec((tm, D), lambda i: (i, 0))
    vec = pl.BlockSpec((1, D), lambda i: (0, 0))
    return pl.pallas_call(
        _out_ln_body,
        grid=(M // tm,),
        in_specs=[row, pl.BlockSpec((D, D), lambda i: (0, 0), pipeline_mode=pl.Buffered(1)), row, vec, vec],
        out_specs=[row, row],
        out_shape=[jax.ShapeDtypeStruct((M, D), F32), jax.ShapeDtypeStruct((M, D), BF16)],
        compiler_params=_params("parallel"),
    )(merged, w, x, g, b)


def _ffn_up_body(x_ref, wu_ref, wg_ref, cw_ref, cb_ref, o_ref, gbuf, *, tiles_per_seq):
    tm = x_ref.shape[0]
    x = x_ref[...]
    up = _dot(x, wu_ref[...])
    g = _dot(x, wg_ref[...])

    @pl.when(pl.program_id(1) % tiles_per_seq == 0)
    def _():
        gbuf[0:TAIL, :] = jnp.zeros((TAIL, gbuf.shape[1]), F32)

    gbuf[TAIL:TAIL + tm, :] = g
    cw = cw_ref[...]
    gt = cb_ref[...] + cw[2:3, :] * g
    for j in range(FFN_CONV - 1):
        gt = gt + cw[j:j + 1, :] * gbuf[pl.ds(TAIL - (FFN_CONV - 1) + j, tm), :]
    gbuf[0:TAIL, :] = gbuf[tm:tm + TAIL, :]
    o_ref[...] = (_silu(gt) * up).astype(o_ref.dtype)


def _ffn_up(xb, wu, wg, cw, cb, seq, tm, tf):
    M, D = xb.shape
    F = wu.shape[1]
    tm, tf = _tile(seq, tm), _tile(F, tf)
    wspec = pl.BlockSpec((D, tf), lambda j, i: (0, j))
    return pl.pallas_call(
        functools.partial(_ffn_up_body, tiles_per_seq=seq // tm),
        grid=(F // tf, M // tm),
        in_specs=[pl.BlockSpec((tm, D), lambda j, i: (i, 0)), wspec, wspec,
                  pl.BlockSpec((FFN_CONV, tf), lambda j, i: (0, j)),
                  pl.BlockSpec((1, tf), lambda j, i: (0, j))],
        out_specs=pl.BlockSpec((tm, tf), lambda j, i: (i, j)),
        out_shape=jax.ShapeDtypeStruct((M, F), BF16),
        scratch_shapes=[pltpu.VMEM((tm + TAIL, tf), F32)],
        compiler_params=_params("arbitrary", "arbitrary"),
    )(xb, wu, wg, cw, cb)


def _ple_body(p_ref, x_ref, wp_ref, wg_ref, o_ref):
    proj = _dot(p_ref[...].astype(BF16), wp_ref[...])
    o_ref[...] = proj * _sigmoid(_dot(x_ref[...], wg_ref[...]))


def _ple(p, xb, wp, wg, tm, tn):
    M, D = xb.shape
    P = p.shape[1]
    tm, tn = _tile(M, tm), _tile(D, tn)
    return pl.pallas_call(
        _ple_body,
        grid=(D // tn, M // tm),
        in_specs=[pl.BlockSpec((tm, P), lambda j, i: (i, 0)), pl.BlockSpec((tm, D), lambda j, i: (i, 0)),
                  pl.BlockSpec((P, tn), lambda j, i: (0, j)), pl.BlockSpec((D, tn), lambda j, i: (0, j))],
        out_specs=pl.BlockSpec((tm, tn), lambda j, i: (i, j)),
        out_shape=jax.ShapeDtypeStruct((M, D), F32),
        compiler_params=_params("parallel", "parallel"),
    )(p, xb, wp, wg)


def _down_ln_body(a_ref, w_ref, ple_ref, x_ref, g_ref, b_ref, o_ref, ob_ref):
    y = DEEPNORM_ALPHA * x_ref[...] + _dot(a_ref[...], w_ref[...]) + ple_ref[...]
    xn = _layer_norm(y, g_ref[...], b_ref[...])
    o_ref[...] = xn
    ob_ref[...] = xn.astype(BF16)


def _down_ln(act, w, ple, x, g, b, tm):
    M, D = x.shape
    F = act.shape[1]
    tm = _tile(M, tm)
    row = pl.BlockSpec((tm, D), lambda i: (i, 0))
    vec = pl.BlockSpec((1, D), lambda i: (0, 0))
    return pl.pallas_call(
        _down_ln_body,
        grid=(M // tm,),
        in_specs=[pl.BlockSpec((tm, F), lambda i: (i, 0)),
                  pl.BlockSpec((F, D), lambda i: (0, 0), pipeline_mode=pl.Buffered(1)), row, row, vec, vec],
        out_specs=[row, row],
        out_shape=[jax.ShapeDtypeStruct((M, D), F32), jax.ShapeDtypeStruct((M, D), BF16)],
        compiler_params=_params("parallel"),
    )(act, w, ple, x, g, b)


def _block_diag(w):
    L, G, gs, _ = w.shape
    per = LRU_BD // gs
    w5 = w.reshape(L, G // per, per, gs, gs)
    eye = jnp.eye(per, dtype=w.dtype)
    return jnp.einsum('lkaij,ab->lkaibj', w5, eye).reshape(L, G // per, LRU_BD, LRU_BD)


def kernel(x, p, w_in, lru_conv_w, lru_conv_b, lru_wr, lru_br, lru_wi, lru_bi, lru_lambda, gla_wg2, gla_bg2, gla_norm_g, gdn_conv_w, gdn_a_log, gdn_dt_bias, gdn_norm_g, w_branch, w_merge, b_merge, w_out, ln1_g, ln1_b, ffn_w_up, ffn_w_gate, ffn_conv_w, ffn_conv_b, ffn_w_down, ple_w_proj, ple_w_gate, ln2_g, ln2_b):
    B, S, D = x.shape
    L = w_in.shape[0]
    M = B * S

    o_lx, o_glr, o_gog, o_dq, o_da, o_dog, o_end = 0, 4096, 4112, 5136, 8208, 8224, 9248
    w_main = jnp.concatenate([w_in[:, :, o_dq:o_da], w_in[:, :, o_lx:o_glr], w_in[:, :, o_gog:o_dq],
                              w_in[:, :, o_dog:o_end]], axis=2).astype(BF16)
    w_small = jnp.concatenate([w_in[:, :, o_glr:o_gog], w_in[:, :, o_da:o_dog],
                               jnp.zeros((L, D, SMALL_COLS - GLA_RANK - 2 * GDN_HEADS), w_in.dtype)], axis=2).astype(BF16)
    nk = LRU_WIDTH // LRU_BD
    lru_wbd = jnp.concatenate([_block_diag(lru_wr), _block_diag(lru_wi)], axis=3).astype(BF16)
    lru_bri = jnp.concatenate([lru_br.reshape(L, nk, 1, LRU_BD), lru_bi.reshape(L, nk, 1, LRU_BD)], axis=3)
    wg2p = jnp.concatenate([gla_wg2, jnp.zeros((L, SMALL_COLS - GLA_RANK, gla_wg2.shape[2]), gla_wg2.dtype)],
                           axis=1).astype(BF16)
    lane_pad = lambda v: jnp.pad(v, ((0, 0), (DA_LANE, SMALL_COLS - DA_LANE - GDN_HEADS)))[:, None, :]
    alog_p, dtb_p = lane_pad(gdn_a_log), lane_pad(gdn_dt_bias)
    wm16, wb16, wo16 = w_merge.astype(BF16), w_branch.astype(BF16), w_out.astype(BF16)
    wu16, wgt16, wd16 = ffn_w_up.astype(BF16), ffn_w_gate.astype(BF16), ffn_w_down.astype(BF16)
    wp16, wpg16 = ple_w_proj.astype(BF16), ple_w_gate.astype(BF16)
    row = lambda v, i: v[i][None, :]

    xf = x.reshape(M, D)
    xb = xf.astype(BF16)
    for i in range(L):
        proj = _matmul(xb, w_main[i], F32, 1024, 1024).reshape(B, S, MAIN_COLS)
        small = _matmul(xb, w_small[i], F32, 2048, SMALL_COLS).reshape(B, S, SMALL_COLS)
        ya = _lru(proj, lru_conv_w[i], row(lru_conv_b, i), lru_wbd[i], lru_bri[i], row(lru_lambda, i), 256)
        yb = _gla(proj, small, wg2p[i], row(gla_bg2, i), row(gla_norm_g, i))
        yc = _gdn(proj, small, gdn_conv_w[i], alog_p[i], dtb_p[i], row(gdn_norm_g, i))
        merged = _merge(xb, ya.reshape(M, -1), yb.reshape(M, -1), yc.reshape(M, -1),
                        wm16[i], wb16[i], b_merge[i][:, None, :], 512, 512)
        xf, xb = _out_ln(merged, wo16[i], xf, row(ln1_g, i), row(ln1_b, i), 256)
        act = _ffn_up(xb, wu16[i], wgt16[i], ffn_conv_w[i], row(ffn_conv_b, i), S, 1024, 512)
        ple = _ple(p[i].reshape(M, PLE_DIM), xb, wp16[i], wpg16[i], 1024, 1024)
        xf, xb = _down_ln(act, wd16[i], ple, xf, row(ln2_g, i), row(ln2_b, i), 256)
    return xf.reshape(B, S, D)
```

```python
import functools

import jax
import jax.numpy as jnp
from jax import lax
from jax.experimental import pallas as pl
from jax.experimental.pallas import tpu as pltpu

F32 = jnp.float32
BF16 = jnp.bfloat16

D_MODEL = 2048
DEPTH = 4
CHUNK = 64
PLE_DIM = 256
LRU_WIDTH = 1024
LRU_GROUPS = 16
LRU_CONV = 4
LRU_C = 8.0
GLA_HEADS = 4
GLA_DK = 128
GLA_DV = 256
GLA_RANK = 16
GLA_GATE_NORM = 16.0
GDN_HEADS = 8
GDN_DK = 128
GDN_DV = 128
GDN_CONV = 4
N_BRANCH = 3
D_FF = 5632
FFN_CONV = 3
LN_EPS = 1e-5
NORM_EPS = 1e-6
DEEPNORM_ALPHA = (2 * DEPTH) ** 0.25

MAIN_COLS = 9216
SMALL_COLS = 128
BLK_LRU_X, BLK_LRU_GATE, BLK_GQK, BLK_GV, BLK_GOG, BLK_DOG = 3, 4, 5, 6, 7, 8
DA_LANE = 16
DB_LANE = 24
LRU_BD = 256
TAIL = 8

VMEM_LIMIT = 56 * 1024 * 1024


def _dot(a, b):
    return jnp.dot(a, b, preferred_element_type=F32)


def _dot_nt(a, b):
    return lax.dot_general(a, b, (((1,), (1,)), ((), ())), preferred_element_type=F32)


def _dot_tn(a, b):
    return lax.dot_general(a, b, (((0,), (0,)), ((), ())), preferred_element_type=F32)


def _split2(x):
    hi = x.astype(BF16)
    lo = (x - hi.astype(F32)).astype(BF16)
    return hi, lo


def _dot_hp(a, b):
    ah, al = _split2(a)
    bh, bl = _split2(b)
    return _dot(ah, bh) + (_dot(ah, bl) + _dot(al, bh))


def _split3(x):
    x1 = x.astype(BF16)
    r = x - x1.astype(F32)
    x2 = r.astype(BF16)
    x3 = (r - x2.astype(F32)).astype(BF16)
    return x1, x2, x3


def _softplus(x):
    return jnp.maximum(x, 0.0) + jnp.log1p(jnp.exp(-jnp.abs(x)))


def _sigmoid(x):
    return 1.0 / (1.0 + jnp.exp(-x))


def _silu(x):
    return x * _sigmoid(x)


def _layer_norm(y, g, b):
    mu = jnp.mean(y, axis=-1, keepdims=True)
    d = y - mu
    var = jnp.mean(d * d, axis=-1, keepdims=True)
    return d * lax.rsqrt(var + LN_EPS) * g + b


def _tile(dim, pref):
    t = min(dim, pref)
    assert dim % t == 0, (dim, pref)
    return t


def _params(*sem):
    return pltpu.CompilerParams(dimension_semantics=sem, vmem_limit_bytes=VMEM_LIMIT)


def _mm_body(x_ref, w_ref, o_ref):
    o_ref[...] = _dot(x_ref[...], w_ref[...]).astype(o_ref.dtype)


def _matmul(x, w, out_dtype, tm, tn):
    M, K = x.shape
    N = w.shape[1]
    tm, tn = _tile(M, tm), _tile(N, tn)
    return pl.pallas_call(
        _mm_body,
        grid=(N // tn, M // tm),
        in_specs=[pl.BlockSpec((tm, K), lambda j, i: (i, 0)),
                  pl.BlockSpec((K, tn), lambda j, i: (0, j))],
        out_specs=pl.BlockSpec((tm, tn), lambda j, i: (i, j)),
        out_shape=jax.ShapeDtypeStruct((M, N), out_dtype),
        compiler_params=_params("parallel", "parallel"),
    )(x, w)


def _lru_body(x_ref, gate_ref, cw_ref, cb_ref, wbd_ref, bri_ref, lam_ref, o_ref, xbuf, hcarry):
    tc = x_ref.shape[1]

    @pl.when(pl.program_id(1) == 0)
    def _():
        xbuf[0:TAIL, :] = jnp.zeros((TAIL, LRU_WIDTH), F32)
        hcarry[...] = jnp.zeros_like(hcarry)

    xbuf[TAIL:TAIL + tc, :] = x_ref[0]
    cw = cw_ref[...]
    xa = cb_ref[...] + cw[3:4, :] * x_ref[0]
    for j in range(LRU_CONV - 1):
        xa = xa + cw[j:j + 1, :] * xbuf[pl.ds(TAIL - (LRU_CONV - 1) + j, tc), :]
    xbuf[0:TAIL, :] = xbuf[tc:tc + TAIL, :]

    lam = lam_ref[...]
    neg_c_sp = -LRU_C * _softplus(-lam)
    row = lax.broadcasted_iota(jnp.int32, (tc, LRU_BD), 0)
    for k in range(LRU_WIDTH // LRU_BD):
        cs = slice(k * LRU_BD, (k + 1) * LRU_BD)
        xk = xa[:, cs]
        ri = _dot(xk.astype(BF16), wbd_ref[k]) + bri_ref[k]
        r = _sigmoid(ri[:, :LRU_BD])
        ig = _sigmoid(ri[:, LRU_BD:])
        log_a = r * neg_c_sp[:, cs]
        a = jnp.exp(log_a)
        u = jnp.sqrt(-jnp.tanh(log_a) * (a * a + 1.0)) * (ig * xk)
        d = 1
        while d < tc:
            keep = row >= d
            a_s = jnp.where(keep, pltpu.roll(a, d, 0), 1.0)
            u_s = jnp.where(keep, pltpu.roll(u, d, 0), 0.0)
            u = a * u_s + u
            a = a * a_s
            d *= 2
        h = u + a * hcarry[:, cs]
        hcarry[:, cs] = h[tc - 1:tc, :]
        o_ref[0, :, cs] = (h * jax.nn.gelu(gate_ref[0, :, cs], approximate=True)).astype(o_ref.dtype)


def _lru(proj, cw, cb, wbd, bri, lam, tc):
    B, S, _ = proj.shape
    tc = _tile(S, tc)
    W = LRU_WIDTH
    nk = W // LRU_BD
    full = lambda *shape: pl.BlockSpec(shape, lambda b, s: (0,) * len(shape))
    return pl.pallas_call(
        _lru_body,
        grid=(B, S // tc),
        in_specs=[pl.BlockSpec((1, tc, W), lambda b, s: (b, s, BLK_LRU_X)),
                  pl.BlockSpec((1, tc, W), lambda b, s: (b, s, BLK_LRU_GATE)),
                  full(LRU_CONV, W), full(1, W), full(nk, LRU_BD, 2 * LRU_BD), full(nk, 1, 2 * LRU_BD), full(1, W)],
        out_specs=pl.BlockSpec((1, tc, W), lambda b, s: (b, s, 0)),
        out_shape=jax.ShapeDtypeStruct((B, S, W), BF16),
        scratch_shapes=[pltpu.VMEM((tc + TAIL, W), F32), pltpu.VMEM((1, W), F32)],
        compiler_params=_params("parallel", "arbitrary"),
    )(proj, proj, cw, cb, wbd, bri, lam)


def _gla_body(qk_ref, v_ref, og_ref, sm_ref, wg2_ref, bg2_ref, ng_ref, o_ref, state):
    H, K, V, C = GLA_HEADS, GLA_DK, GLA_DV, CHUNK

    @pl.when(pl.program_id(1) == 0)
    def _():
        state[...] = jnp.zeros_like(state)

    z = _dot(sm_ref[0].astype(BF16), wg2_ref[...]) + bg2_ref[...]
    fg = (jnp.minimum(z, 0.0) - jnp.log1p(jnp.exp(-jnp.abs(z)))) * (1.0 / GLA_GATE_NORM)
    ti = lax.broadcasted_iota(jnp.int32, (C, C), 0)
    si = lax.broadcasted_iota(jnp.int32, (C, C), 1)
    causal = ti >= si
    tri = jnp.where(causal, 1.0, 0.0).astype(BF16)
    f1, f2, f3 = _split3(fg)
    bcum = _dot(tri, f1) + (_dot(tri, f2) + _dot(tri, f3))
    ng = ng_ref[...]
    bl = bcum[C - 1:C, :]
    qk = qk_ref[0]
    q_all = qk[:, :H * K] * (K ** -0.5)
    k_all = qk[:, H * K:]
    q_dec_all = (q_all * jnp.exp(bcum)).astype(BF16)
    k_neg_all = (k_all * jnp.exp(-bcum)).astype(BF16)
    k_dec_all = (k_all * jnp.exp(bl - bcum)).astype(BF16)
    ebl = jnp.exp(bl)
    heads = range(H)
    ks = [slice(h * K, (h + 1) * K) for h in heads]
    vs = [slice(h * V, (h + 1) * V) for h in heads]
    v = [v_ref[0, :, vs[h]].astype(BF16) for h in heads]
    scores = [jnp.where(causal, _dot_nt(q_dec_all[:, ks[h]], k_neg_all[:, ks[h]]), 0.0).astype(BF16) for h in heads]
    st = [state[h] for h in heads]
    o = [_dot(scores[h], v[h]) + _dot_nt(q_dec_all[:, ks[h]], st[h].astype(BF16)) for h in heads]
    for h in heads:
        state[h] = st[h] * ebl[:, ks[h]] + _dot_tn(v[h], k_dec_all[:, ks[h]])
    for h in heads:
        on = o[h] * lax.rsqrt(jnp.mean(o[h] * o[h], axis=-1, keepdims=True) + NORM_EPS) * ng
        o_ref[0, :, vs[h]] = (on * _silu(og_ref[0, :, vs[h]])).astype(o_ref.dtype)


def _gla(proj, small, wg2p, bg2, ng):
    B, S, _ = proj.shape
    C = CHUNK
    W = GLA_HEADS * GLA_DV
    full = lambda *shape: pl.BlockSpec(shape, lambda b, s: (0,) * len(shape))
    return pl.pallas_call(
        _gla_body,
        grid=(B, S // C),
        in_specs=[pl.BlockSpec((1, C, W), lambda b, s: (b, s, BLK_GQK)),
                  pl.BlockSpec((1, C, W), lambda b, s: (b, s, BLK_GV)),
                  pl.BlockSpec((1, C, W), lambda b, s: (b, s, BLK_GOG)),
                  pl.BlockSpec((1, C, SMALL_COLS), lambda b, s: (b, s, 0)),
                  full(SMALL_COLS, GLA_HEADS * GLA_DK), full(1, GLA_HEADS * GLA_DK), full(1, GLA_DV)],
        out_specs=pl.BlockSpec((1, C, W), lambda b, s: (b, s, 0)),
        out_shape=jax.ShapeDtypeStruct((B, S, W), BF16),
        scratch_shapes=[pltpu.VMEM((GLA_HEADS, GLA_DV, GLA_DK), F32)],
        compiler_params=_params("parallel", "arbitrary"),
    )(proj, proj, proj, small, wg2p, bg2, ng)


def _unit_lower_inverses(n_list, ti, si):
    C = n_list[0].shape[0]
    eye = jnp.where(ti == si, 1.0, 0.0)
    pair = ti // 2 == si // 2
    ts = [eye - jnp.where(pair, n, 0.0) for n in n_list]
    d = 2
    while d < C:
        m = (ti // (2 * d) == si // (2 * d)) & (ti % (2 * d) >= d) & (si % (2 * d) < d)
        cds = [jnp.where(m, n, 0.0).astype(BF16) for n in n_list]
        tbs = [t.astype(BF16) for t in ts]
        tcs = [_dot(tb, cd).astype(BF16) for tb, cd in zip(tbs, cds)]
        ts = [t - _dot(tc, tb) for t, tc, tb in zip(ts, tcs, tbs)]
        d *= 2
    return ts


def _gdn_body(qkv_ref, og_ref, sm_ref, cw_ref, alog_ref, dtb_ref, ng_ref, o_ref, xbuf, state):
    H, K, V, C = GDN_HEADS, GDN_DK, GDN_DV, CHUNK
    QKV = qkv_ref.shape[2]

    @pl.when(pl.program_id(1) == 0)
    def _():
        xbuf[0:TAIL, :] = jnp.zeros((TAIL, QKV), F32)
        state[...] = jnp.zeros_like(state)

    xbuf[TAIL:TAIL + C, :] = qkv_ref[0]
    cw = cw_ref[...]
    conv = cw[3:4, :] * qkv_ref[0]
    for j in range(GDN_CONV - 1):
        conv = conv + cw[j:j + 1, :] * xbuf[pl.ds(TAIL - (GDN_CONV - 1) + j, C), :]
    xbuf[0:TAIL, :] = xbuf[C:C + TAIL, :]
    qkv = _silu(conv)

    sm = sm_ref[0]
    beta_all = _sigmoid(sm)
    g_all = -jnp.exp(alog_ref[...]) * _softplus(sm + dtb_ref[...])
    ti = lax.broadcasted_iota(jnp.int32, (C, C), 0)
    si = lax.broadcasted_iota(jnp.int32, (C, C), 1)
    incl = ti >= si
    strict = ti > si
    tri = jnp.where(incl, 1.0, 0.0).astype(BF16)
    g1, g2, g3 = _split3(g_all)
    gam = _dot(tri, g1) + (_dot(tri, g2) + _dot(tri, g3))
    sel = jnp.where(lax.broadcasted_iota(jnp.int32, (TAIL, SMALL_COLS), 1)
                    == lax.broadcasted_iota(jnp.int32, (TAIL, SMALL_COLS), 0) + DA_LANE, 1.0, 0.0).astype(BF16)
    m1, m2, m3 = _split3(gam)
    gam_rows = _dot_nt(sel, m1) + (_dot_nt(sel, m2) + _dot_nt(sel, m3))
    egam = jnp.exp(gam)
    gl_row = gam[C - 1:C, :]
    ekd = jnp.exp(gl_row - gam)
    elast = jnp.exp(gl_row)
    ng = ng_ref[...]

    heads = range(H)
    gcol = [gam[:, DA_LANE + h:DA_LANE + h + 1] for h in heads]
    bt = [beta_all[:, DB_LANE + h:DB_LANE + h + 1] for h in heads]
    eg = [egam[:, DA_LANE + h:DA_LANE + h + 1] for h in heads]
    decay = [jnp.where(incl, jnp.exp(jnp.where(incl, gcol[h] - gam_rows[h:h + 1, :], 0.0)), 0.0) for h in heads]
    q = [qkv[:, h * K:(h + 1) * K] for h in heads]
    k = [qkv[:, (H + h) * K:(H + h + 1) * K] for h in heads]
    v = [qkv[:, 2 * H * K + h * V:2 * H * K + (h + 1) * V] for h in heads]
    q = [x * (lax.rsqrt(jnp.sum(x * x, axis=-1, keepdims=True) + NORM_EPS) * (K ** -0.5)) for x in q]
    k = [x * lax.rsqrt(jnp.sum(x * x, axis=-1, keepdims=True) + NORM_EPS) for x in k]
    kb = [k[h] * bt[h] for h in heads]
    k16 = [x.astype(BF16) for x in k]
    kk = [_dot_nt(kb[h].astype(BF16), k16[h]) * decay[h] for h in heads]
    qk = [(_dot_nt(q[h].astype(BF16), k16[h]) * decay[h]).astype(BF16) for h in heads]
    t_inv = _unit_lower_inverses([jnp.where(strict, x, 0.0) for x in kk], ti, si)
    t16 = [t.astype(BF16) for t in t_inv]
    rhs = [jnp.concatenate([v[h] * bt[h], kb[h] * eg[h]], axis=1).astype(BF16) for h in heads]
    sol = [_dot(t16[h], rhs[h]) for h in heads]
    q_dec = [(q[h] * eg[h]).astype(BF16) for h in heads]
    k_dec = [(k[h] * ekd[:, DA_LANE + h:DA_LANE + h + 1]).astype(BF16) for h in heads]
    st = [state[h] for h in heads]
    st16 = [s.astype(BF16) for s in st]
    v_new = [sol[h][:, :V] - _dot(sol[h][:, V:].astype(BF16), st16[h]) for h in heads]
    vn16 = [x.astype(BF16) for x in v_new]
    o = [_dot(q_dec[h], st16[h]) + _dot(qk[h], vn16[h]) for h in heads]
    for h in heads:
        state[h] = st[h] * elast[:, DA_LANE + h:DA_LANE + h + 1] + _dot_tn(k_dec[h], vn16[h])
    for h in heads:
        on = o[h] * lax.rsqrt(jnp.mean(o[h] * o[h], axis=-1, keepdims=True) + NORM_EPS) * ng
        o_ref[0, :, h * V:(h + 1) * V] = (on * _silu(og_ref[0, :, h * V:(h + 1) * V])).astype(o_ref.dtype)


def _gdn(proj, small, cw, alog_p, dtb_p, ng):
    B, S, _ = proj.shape
    C = CHUNK
    QKV = 2 * GDN_HEADS * GDN_DK + GDN_HEADS * GDN_DV
    W = GDN_HEADS * GDN_DV
    full = lambda *shape: pl.BlockSpec(shape, lambda b, s: (0,) * len(shape))
    return pl.pallas_call(
        _gdn_body,
        grid=(B, S // C),
        in_specs=[pl.BlockSpec((1, C, QKV), lambda b, s: (b, s, 0)),
                  pl.BlockSpec((1, C, W), lambda b, s: (b, s, BLK_DOG)),
                  pl.BlockSpec((1, C, SMALL_COLS), lambda b, s: (b, s, 0)),
                  full(GDN_CONV, QKV), full(1, SMALL_COLS), full(1, SMALL_COLS), full(1, GDN_DV)],
        out_specs=pl.BlockSpec((1, C, W), lambda b, s: (b, s, 0)),
        out_shape=jax.ShapeDtypeStruct((B, S, W), BF16),
        scratch_shapes=[pltpu.VMEM((C + TAIL, QKV), F32), pltpu.VMEM((GDN_HEADS, GDN_DK, GDN_DV), F32)],
        compiler_params=_params("parallel", "arbitrary"),
    )(proj, proj, small, cw, alog_p, dtb_p, ng)


def _merge_body(h_ref, ya_ref, yb_ref, yc_ref, wm_ref, wb_ref, bm_ref, o_ref):
    hx = h_ref[...]
    acc = None
    for n, y_ref in enumerate((ya_ref, yb_ref, yc_ref)):
        gate = _sigmoid(_dot(hx, wm_ref[n]) + bm_ref[n])
        term = gate * _dot(y_ref[...], wb_ref[n])
        acc = term if acc is None else acc + term
    o_ref[...] = acc.astype(o_ref.dtype)


def _merge(hb, ya, yb, yc, wm, wb, bm, tm, tn):
    M, D = hb.shape
    Wb = ya.shape[1]
    tm, tn = _tile(M, tm), _tile(D, tn)
    ybs = pl.BlockSpec((tm, Wb), lambda j, i: (i, 0))
    return pl.pallas_call(
        _merge_body,
        grid=(D // tn, M // tm),
        in_specs=[pl.BlockSpec((tm, D), lambda j, i: (i, 0)), ybs, ybs, ybs,
                  pl.BlockSpec((N_BRANCH, D, tn), lambda j, i: (0, 0, j)),
                  pl.BlockSpec((N_BRANCH, Wb, tn), lambda j, i: (0, 0, j)),
                  pl.BlockSpec((N_BRANCH, 1, tn), lambda j, i: (0, 0, j))],
        out_specs=pl.BlockSpec((tm, tn), lambda j, i: (i, j)),
        out_shape=jax.ShapeDtypeStruct((M, D), BF16),
        compiler_params=_params("parallel", "parallel"),
    )(hb, ya, yb, yc, wm, wb, bm)


def _out_ln_body(m_ref, w_ref, x_ref, g_ref, b_ref, o_ref, ob_ref):
    y = DEEPNORM_ALPHA * x_ref[...] + _dot(m_ref[...], w_ref[...])
    xn = _layer_norm(y, g_ref[...], b_ref[...])
    o_ref[...] = xn
    ob_ref[...] = xn.astype(BF16)


def _out_ln(merged, w, x, g, b, tm):
    M, D = x.shape
    tm = _tile(M, tm)
    row = pl.BlockSpec((tm, D), lambda i: (i, 0))
    vec = pl.BlockSpec((1, D), lambda i: (0, 0))
    return pl.pallas_call(
        _out_ln_body,
        grid=(M // tm,),
        in_specs=[row, pl.BlockSpec((D, D), lambda i: (0, 0), pipeline_mode=pl.Buffered(1)), row, vec, vec],
        out_specs=[row, row],
        out_shape=[jax.ShapeDtypeStruct((M, D), F32), jax.ShapeDtypeStruct((M, D), BF16)],
        compiler_params=_params("parallel"),
    )(merged, w, x, g, b)


def _ffn_up_body(x_ref, wu_ref, wg_ref, cw_ref, cb_ref, o_ref, gbuf, *, tiles_per_seq):
    tm = x_ref.shape[0]
    x = x_ref[...]
    up = _dot(x, wu_ref[...])
    g = _dot(x, wg_ref[...])

    @pl.when(pl.program_id(1) % tiles_per_seq == 0)
    def _():
        gbuf[0:TAIL, :] = jnp.zeros((TAIL, gbuf.shape[1]), F32)

    gbuf[TAIL:TAIL + tm, :] = g
    cw = cw_ref[...]
    gt = cb_ref[...] + cw[2:3, :] * g
    for j in range(FFN_CONV - 1):
        gt = gt + cw[j:j + 1, :] * gbuf[pl.ds(TAIL - (FFN_CONV - 1) + j, tm), :]
    gbuf[0:TAIL, :] = gbuf[tm:tm + TAIL, :]
    o_ref[...] = (_silu(gt) * up).astype(o_ref.dtype)


def _ffn_up(xb, wu, wg, cw, cb, seq, tm, tf):
    M, D = xb.shape
    F = wu.shape[1]
    tm, tf = _tile(seq, tm), _tile(F, tf)
    wspec = pl.BlockSpec((D, tf), lambda j, i: (0, j))
    return pl.pallas_call(
        functools.partial(_ffn_up_body, tiles_per_seq=seq // tm),
        grid=(F // tf, M // tm),
        in_specs=[pl.BlockSpec((tm, D), lambda j, i: (i, 0)), wspec, wspec,
                  pl.BlockSpec((FFN_CONV, tf), lambda j, i: (0, j)),
                  pl.BlockSpec((1, tf), lambda j, i: (0, j))],
        out_specs=pl.BlockSpec((tm, tf), lambda j, i: (i, j)),
        out_shape=jax.ShapeDtypeStruct((M, F), BF16),
        scratch_shapes=[pltpu.VMEM((tm + TAIL, tf), F32)],
        compiler_params=_params("arbitrary", "arbitrary"),
    )(xb, wu, wg, cw, cb)


def _ple_body(p_ref, x_ref, wp_ref, wg_ref, o_ref):
    proj = _dot(p_ref[...].astype(BF16), wp_ref[...])
    o_ref[...] = proj * _sigmoid(_dot(x_ref[...], wg_ref[...]))


def _ple(p, xb, wp, wg, tm, tn):
    M, D = xb.shape
    P = p.shape[1]
    tm, tn = _tile(M, tm), _tile(D, tn)
    return pl.pallas_call(
        _ple_body,
        grid=(D // tn, M // tm),
        in_specs=[pl.BlockSpec((tm, P), lambda j, i: (i, 0)), pl.BlockSpec((tm, D), lambda j, i: (i, 0)),
                  pl.BlockSpec((P, tn), lambda j, i: (0, j)), pl.BlockSpec((D, tn), lambda j, i: (0, j))],
        out_specs=pl.BlockSpec((tm, tn), lambda j, i: (i, j)),
        out_shape=jax.ShapeDtypeStruct((M, D), F32),
        compiler_params=_params("parallel", "parallel"),
    )(p, xb, wp, wg)


def _down_ln_body(a_ref, w_ref, ple_ref, x_ref, g_ref, b_ref, o_ref, ob_ref):
    y = DEEPNORM_ALPHA * x_ref[...] + _dot(a_ref[...], w_ref[...]) + ple_ref[...]
    xn = _layer_norm(y, g_ref[...], b_ref[...])
    o_ref[...] = xn
    ob_ref[...] = xn.astype(BF16)


def _down_ln(act, w, ple, x, g, b, tm):
    M, D = x.shape
    F = act.shape[1]
    tm = _tile(M, tm)
    row = pl.BlockSpec((tm, D), lambda i: (i, 0))
    vec = pl.BlockSpec((1, D), lambda i: (0, 0))
    return pl.pallas_call(
        _down_ln_body,
        grid=(M // tm,),
        in_specs=[pl.BlockSpec((tm, F), lambda i: (i, 0)),
                  pl.BlockSpec((F, D), lambda i: (0, 0), pipeline_mode=pl.Buffered(1)), row, row, vec, vec],
        out_specs=[row, row],
        out_shape=[jax.ShapeDtypeStruct((M, D), F32), jax.ShapeDtypeStruct((M, D), BF16)],
        compiler_params=_params("parallel"),
    )(act, w, ple, x, g, b)


def _block_diag(w):
    L, G, gs, _ = w.shape
    per = LRU_BD // gs
    w5 = w.reshape(L, G // per, per, gs, gs)
    eye = jnp.eye(per, dtype=w.dtype)
    return jnp.einsum('lkaij,ab->lkaibj', w5, eye).reshape(L, G // per, LRU_BD, LRU_BD)


def kernel(x, p, w_in, lru_conv_w, lru_conv_b, lru_wr, lru_br, lru_wi, lru_bi, lru_lambda, gla_wg2, gla_bg2, gla_norm_g, gdn_conv_w, gdn_a_log, gdn_dt_bias, gdn_norm_g, w_branch, w_merge, b_merge, w_out, ln1_g, ln1_b, ffn_w_up, ffn_w_gate, ffn_conv_w, ffn_conv_b, ffn_w_down, ple_w_proj, ple_w_gate, ln2_g, ln2_b):
    B, S, D = x.shape
    L = w_in.shape[0]
    M = B * S

    o_lx, o_glr, o_gog, o_dq, o_da, o_dog, o_end = 0, 4096, 4112, 5136, 8208, 8224, 9248
    w_main = jnp.concatenate([w_in[:, :, o_dq:o_da], w_in[:, :, o_lx:o_glr], w_in[:, :, o_gog:o_dq],
                              w_in[:, :, o_dog:o_end]], axis=2).astype(BF16)
    w_small = jnp.concatenate([w_in[:, :, o_glr:o_gog], w_in[:, :, o_da:o_dog],
                               jnp.zeros((L, D, SMALL_COLS - GLA_RANK - 2 * GDN_HEADS), w_in.dtype)], axis=2).astype(BF16)
    nk = LRU_WIDTH // LRU_BD
    lru_wbd = jnp.concatenate([_block_diag(lru_wr), _block_diag(lru_wi)], axis=3).astype(BF16)
    lru_bri = jnp.concatenate([lru_br.reshape(L, nk, 1, LRU_BD), lru_bi.reshape(L, nk, 1, LRU_BD)], axis=3)
    wg2p = jnp.concatenate([gla_wg2, jnp.zeros((L, SMALL_COLS - GLA_RANK, gla_wg2.shape[2]), gla_wg2.dtype)],
                           axis=1).astype(BF16)
    lane_pad = lambda v: jnp.pad(v, ((0, 0), (DA_LANE, SMALL_COLS - DA_LANE - GDN_HEADS)))[:, None, :]
    alog_p, dtb_p = lane_pad(gdn_a_log), lane_pad(gdn_dt_bias)
    wm16, wb16, wo16 = w_merge.astype(BF16), w_branch.astype(BF16), w_out.astype(BF16)
    wu16, wgt16, wd16 = ffn_w_up.astype(BF16), ffn_w_gate.astype(BF16), ffn_w_down.astype(BF16)
    wp16, wpg16 = ple_w_proj.astype(BF16), ple_w_gate.astype(BF16)
    row = lambda v, i: v[i][None, :]

    xf = x.reshape(M, D)
    xb = xf.astype(BF16)
    for i in range(L):
        proj = _matmul(xb, w_main[i], F32, 1024, 1024).reshape(B, S, MAIN_COLS)
        small = _matmul(xb, w_small[i], F32, 2048, SMALL_COLS).reshape(B, S, SMALL_COLS)
        ya = _lru(proj, lru_conv_w[i], row(lru_conv_b, i), lru_wbd[i], lru_bri[i], row(lru_lambda, i), 256)
        yb = _gla(proj, small, wg2p[i], row(gla_bg2, i), row(gla_norm_g, i))
        yc = _gdn(proj, small, gdn_conv_w[i], alog_p[i], dtb_p[i], row(gdn_norm_g, i))
        merged = _merge(xb, ya.reshape(M, -1), yb.reshape(M, -1), yc.reshape(M, -1),
                        wm16[i], wb16[i], b_merge[i][:, None, :], 512, 512)
        xf, xb = _out_ln(merged, wo16[i], xf, row(ln1_g, i), row(ln1_b, i), 256)
        act = _ffn_up(xb, wu16[i], wgt16[i], ffn_conv_w[i], row(ffn_conv_b, i), S, 1024, 512)
        ple = _ple(p[i].reshape(M, PLE_DIM), xb, wp16[i], wpg16[i], 1024, 1024)
        xf, xb = _down_ln(act, wd16[i], ple, xf, row(ln2_g, i), row(ln2_b, i), 256)
    return xf.reshape(B, S, D)
```

```python
import functools

import jax
import jax.numpy as jnp
from jax import lax
from jax.experimental import pallas as pl
from jax.experimental.pallas import tpu as pltpu

F32 = jnp.float32
BF16 = jnp.bfloat16

D_MODEL = 2048
DEPTH = 4
CHUNK = 64
PLE_DIM = 256
LRU_WIDTH = 1024
LRU_GROUPS = 16
LRU_CONV = 4
LRU_C = 8.0
GLA_HEADS = 4
GLA_DK = 128
GLA_DV = 256
GLA_RANK = 16
GLA_GATE_NORM = 16.0
GDN_HEADS = 8
GDN_DK = 128
GDN_DV = 128
GDN_CONV = 4
N_BRANCH = 3
D_FF = 5632
FFN_CONV = 3
LN_EPS = 1e-5
NORM_EPS = 1e-6
DEEPNORM_ALPHA = (2 * DEPTH) ** 0.25

MAIN_COLS = 9216
SMALL_COLS = 128
BLK_LRU_X, BLK_LRU_GATE, BLK_GQK, BLK_GV, BLK_GOG, BLK_DOG = 3, 4, 5, 6, 7, 8
DA_LANE = 16
DB_LANE = 24
LRU_BD = 256
SUBLANES = 8
TAIL = SUBLANES

VMEM_LIMIT = 56 * 1024 * 1024


def _dot(a, b):
    return jnp.dot(a, b, preferred_element_type=F32)


def _dot_nt(a, b):
    return lax.dot_general(a, b, (((1,), (1,)), ((), ())), preferred_element_type=F32)


def _dot_tn(a, b):
    return lax.dot_general(a, b, (((0,), (0,)), ((), ())), preferred_element_type=F32)


def _split2(x):
    hi = x.astype(BF16)
    lo = (x - hi.astype(F32)).astype(BF16)
    return hi, lo


def _dot_hp(a, b):
    ah, al = _split2(a)
    bh, bl = _split2(b)
    return _dot(ah, bh) + (_dot(ah, bl) + _dot(al, bh))


def _split3(x):
    x1 = x.astype(BF16)
    r = x - x1.astype(F32)
    x2 = r.astype(BF16)
    x3 = (r - x2.astype(F32)).astype(BF16)
    return x1, x2, x3


def _softplus(x):
    return jnp.maximum(x, 0.0) + jnp.log1p(jnp.exp(-jnp.abs(x)))


def _sigmoid(x):
    return 1.0 / (1.0 + jnp.exp(-x))


def _silu(x):
    return x * _sigmoid(x)


def _layer_norm(y, g, b):
    mu = jnp.mean(y, axis=-1, keepdims=True)
    d = y - mu
    var = jnp.mean(d * d, axis=-1, keepdims=True)
    return d * lax.rsqrt(var + LN_EPS) * g + b


def _tile(dim, pref):
    t = min(dim, pref)
    assert dim % t == 0, (dim, pref)
    return t


def _params(*sem):
    return pltpu.CompilerParams(dimension_semantics=sem, vmem_limit_bytes=VMEM_LIMIT)


def _mm_body(x_ref, w_ref, o_ref):
    o_ref[...] = _dot(x_ref[...], w_ref[...]).astype(o_ref.dtype)


def _lspec(layer, shape, index_map=None):
    if index_map is None:
        index_map = lambda *g: (0,) * len(shape)
    return pl.BlockSpec((None,) + tuple(shape), lambda *g: (layer,) + tuple(index_map(*g)))


def _matmul(x, w, layer, out_dtype, tm, tn):
    M, K = x.shape
    N = w.shape[2]
    tm, tn = _tile(M, tm), _tile(N, tn)
    return pl.pallas_call(
        _mm_body,
        grid=(N // tn, M // tm),
        in_specs=[pl.BlockSpec((tm, K), lambda j, i: (i, 0)),
                  _lspec(layer, (K, tn), lambda j, i: (0, j))],
        out_specs=pl.BlockSpec((tm, tn), lambda j, i: (i, j)),
        out_shape=jax.ShapeDtypeStruct((M, N), out_dtype),
        compiler_params=_params("parallel", "parallel"),
    )(x, w)


def _lru_body(x_ref, gate_ref, cw_ref, cb_ref, wbd_ref, bri_ref, lam_ref, o_ref, xbuf, hcarry):
    tc = x_ref.shape[1]

    @pl.when(pl.program_id(1) == 0)
    def _():
        xbuf[0:TAIL, :] = jnp.zeros((TAIL, LRU_WIDTH), F32)
        hcarry[...] = jnp.zeros_like(hcarry)

    xbuf[TAIL:TAIL + tc, :] = x_ref[0]
    cw = cw_ref[...]
    xa = cb_ref[...] + cw[3:4, :] * x_ref[0]
    for j in range(LRU_CONV - 1):
        xa = xa + cw[j:j + 1, :] * xbuf[pl.ds(TAIL - (LRU_CONV - 1) + j, tc), :]
    xbuf[0:TAIL, :] = xbuf[tc:tc + TAIL, :]

    lam = lam_ref[...]
    neg_c_sp = -LRU_C * _softplus(-lam)
    sub = lax.broadcasted_iota(jnp.int32, (tc // SUBLANES, SUBLANES, LRU_BD), 1)
    for k in range(LRU_WIDTH // LRU_BD):
        cs = slice(k * LRU_BD, (k + 1) * LRU_BD)
        xk = xa[:, cs]
        ri = _dot(xk.astype(BF16), wbd_ref[k]) + bri_ref[k]
        r = _sigmoid(ri[:, :LRU_BD])
        ig = _sigmoid(ri[:, LRU_BD:])
        log_a = r * neg_c_sp[:, cs]
        a = jnp.exp(log_a)
        u = jnp.sqrt(-jnp.tanh(log_a) * (a * a + 1.0)) * (ig * xk)
        a = a.reshape(tc // SUBLANES, SUBLANES, LRU_BD)
        u = u.reshape(tc // SUBLANES, SUBLANES, LRU_BD)
        d = 1
        while d < SUBLANES:
            keep = sub >= d
            a_s = jnp.where(keep, pltpu.roll(a, d, 1), 1.0)
            u_s = jnp.where(keep, pltpu.roll(u, d, 1), 0.0)
            u = a * u_s + u
            a = a * a_s
            d *= 2
        hc = hcarry[:, cs]
        hs = []
        for g in range(tc // SUBLANES):
            hg = u[g] + a[g] * hc
            hs.append(hg)
            hc = hg[SUBLANES - 1:SUBLANES, :]
        h = jnp.concatenate(hs, axis=0)
        hcarry[:, cs] = hc
        o_ref[0, :, cs] = (h * jax.nn.gelu(gate_ref[0, :, cs], approximate=True)).astype(o_ref.dtype)


def _lru(proj, cw, cb, wbd, bri, lam, layer, tc):
    B, S, _ = proj.shape
    tc = _tile(S, tc)
    W = LRU_WIDTH
    nk = W // LRU_BD
    full = lambda *shape: _lspec(layer, shape)
    return pl.pallas_call(
        _lru_body,
        grid=(B, S // tc),
        in_specs=[pl.BlockSpec((1, tc, W), lambda b, s: (b, s, BLK_LRU_X)),
                  pl.BlockSpec((1, tc, W), lambda b, s: (b, s, BLK_LRU_GATE)),
                  full(LRU_CONV, W), full(1, W), full(nk, LRU_BD, 2 * LRU_BD), full(nk, 1, 2 * LRU_BD), full(1, W)],
        out_specs=pl.BlockSpec((1, tc, W), lambda b, s: (b, s, 0)),
        out_shape=jax.ShapeDtypeStruct((B, S, W), BF16),
        scratch_shapes=[pltpu.VMEM((tc + TAIL, W), F32), pltpu.VMEM((1, W), F32)],
        compiler_params=_params("parallel", "arbitrary"),
    )(proj, proj, cw, cb, wbd, bri, lam)


def _gla_body(qk_ref, v_ref, og_ref, sm_ref, wg2_ref, bg2_ref, ng_ref, o_ref, state):
    H, K, V, C = GLA_HEADS, GLA_DK, GLA_DV, CHUNK
    T = qk_ref.shape[1]
    chunks = range(T // C)
    heads = range(H)
    z = _dot(sm_ref[0].astype(BF16), wg2_ref[...]) + bg2_ref[...]
    fg = (jnp.minimum(z, 0.0) - jnp.log1p(jnp.exp(-jnp.abs(z)))) * (1.0 / GLA_GATE_NORM)
    tt = lax.broadcasted_iota(jnp.int32, (T, T), 0)
    ss = lax.broadcasted_iota(jnp.int32, (T, T), 1)
    tri = jnp.where((tt >= ss) & (tt // C == ss // C), 1.0, 0.0).astype(BF16)
    f1, f2, f3 = _split3(fg)
    bcum = _dot(tri, f1) + (_dot(tri, f2) + _dot(tri, f3))
    causal = lax.broadcasted_iota(jnp.int32, (C, C), 0) >= lax.broadcasted_iota(jnp.int32, (C, C), 1)
    ng = ng_ref[...]
    qk = qk_ref[0]
    q_all = qk[:, :H * K] * (K ** -0.5)
    k_all = qk[:, H * K:]
    q_dec_all = (q_all * jnp.exp(bcum)).astype(BF16)
    k_neg_all = (k_all * jnp.exp(-bcum)).astype(BF16)
    rows = [slice(c * C, (c + 1) * C) for c in chunks]
    bl = [bcum[(c + 1) * C - 1:(c + 1) * C, :] for c in chunks]
    k_dec = [(k_all[rows[c], :] * jnp.exp(bl[c] - bcum[rows[c], :])).astype(BF16) for c in chunks]
    ebl = [jnp.exp(bl[c]) for c in chunks]
    ks = [slice(h * K, (h + 1) * K) for h in heads]
    vs = [slice(h * V, (h + 1) * V) for h in heads]
    v = [[v_ref[0, rows[c], vs[h]].astype(BF16) for h in heads] for c in chunks]
    scores = [[jnp.where(causal, _dot_nt(q_dec_all[rows[c], ks[h]], k_neg_all[rows[c], ks[h]]), 0.0).astype(BF16)
               for h in heads] for c in chunks]
    intra = [[_dot(scores[c][h], v[c][h]) for h in heads] for c in chunks]
    st = [state[h] for h in heads]
    for c in chunks:
        o = [intra[c][h] + _dot_nt(q_dec_all[rows[c], ks[h]], st[h].astype(BF16)) for h in heads]
        st = [st[h] * ebl[c][:, ks[h]] + _dot_tn(v[c][h], k_dec[c][:, ks[h]]) for h in heads]
        for h in heads:
            on = o[h] * lax.rsqrt(jnp.mean(o[h] * o[h], axis=-1, keepdims=True) + NORM_EPS) * ng
            o_ref[0, rows[c], vs[h]] = (on * _silu(og_ref[0, rows[c], vs[h]])).astype(o_ref.dtype)
    for h in heads:
        state[h] = st[h]


def _unit_lower_inverses(n_list, ti, si):
    C = n_list[0].shape[0]
    eye = jnp.where(ti == si, 1.0, 0.0)
    pair = ti // 2 == si // 2
    ts = [eye - jnp.where(pair, n, 0.0) for n in n_list]
    d = 2
    while d < C:
        m = (ti // (2 * d) == si // (2 * d)) & (ti % (2 * d) >= d) & (si % (2 * d) < d)
        cds = [jnp.where(m, n, 0.0).astype(BF16) for n in n_list]
        tbs = [t.astype(BF16) for t in ts]
        tcs = [_dot(tb, cd).astype(BF16) for tb, cd in zip(tbs, cds)]
        ts = [t - _dot(tc, tb) for t, tc, tb in zip(ts, tcs, tbs)]
        d *= 2
    return ts


def _gdn_body(qkv_ref, og_ref, sm_ref, cw_ref, alog_ref, dtb_ref, ng_ref, o_ref, xbuf, state):
    H, K, V, C = GDN_HEADS, GDN_DK, GDN_DV, CHUNK
    T = qkv_ref.shape[1]
    chunks = range(T // C)
    heads = range(H)
    xbuf[TAIL:TAIL + T, :] = qkv_ref[0]
    cw = cw_ref[...]
    conv = cw[3:4, :] * qkv_ref[0]
    for j in range(GDN_CONV - 1):
        conv = conv + cw[j:j + 1, :] * xbuf[pl.ds(TAIL - (GDN_CONV - 1) + j, T), :]
    xbuf[0:TAIL, :] = xbuf[T:T + TAIL, :]
    qkv = _silu(conv)

    sm = sm_ref[0]
    beta_all = _sigmoid(sm)
    g_all = -jnp.exp(alog_ref[...]) * _softplus(sm + dtb_ref[...])
    tt = lax.broadcasted_iota(jnp.int32, (T, T), 0)
    ss = lax.broadcasted_iota(jnp.int32, (T, T), 1)
    tri = jnp.where((tt >= ss) & (tt // C == ss // C), 1.0, 0.0).astype(BF16)
    g1, g2, g3 = _split3(g_all)
    gam = _dot(tri, g1) + (_dot(tri, g2) + _dot(tri, g3))
    sel = jnp.where(lax.broadcasted_iota(jnp.int32, (SUBLANES, SMALL_COLS), 1)
                    == lax.broadcasted_iota(jnp.int32, (SUBLANES, SMALL_COLS), 0) + DA_LANE, 1.0, 0.0).astype(BF16)
    m1, m2, m3 = _split3(gam)
    gam_rows = _dot_nt(sel, m1) + (_dot_nt(sel, m2) + _dot_nt(sel, m3))
    egam = jnp.exp(gam)
    ti = lax.broadcasted_iota(jnp.int32, (C, C), 0)
    si = lax.broadcasted_iota(jnp.int32, (C, C), 1)
    incl = ti >= si
    strict = ti > si
    ng = ng_ref[...]
    rows = [slice(c * C, (c + 1) * C) for c in chunks]
    gl_row = [gam[(c + 1) * C - 1:(c + 1) * C, :] for c in chunks]
    ekd = [jnp.exp(gl_row[c] - gam[rows[c], :]) for c in chunks]
    elast = [jnp.exp(gl_row[c]) for c in chunks]
    qn = [qkv[:, h * K:(h + 1) * K] for h in heads]
    kn = [qkv[:, (H + h) * K:(H + h + 1) * K] for h in heads]
    qn = [x * (lax.rsqrt(jnp.sum(x * x, axis=-1, keepdims=True) + NORM_EPS) * (K ** -0.5)) for x in qn]
    kn = [x * lax.rsqrt(jnp.sum(x * x, axis=-1, keepdims=True) + NORM_EPS) for x in kn]

    pairs = [(c, h) for c in chunks for h in heads]
    lane = lambda x, c, h, base: x[rows[c], base + h:base + h + 1]
    gcol = [lane(gam, c, h, DA_LANE) for c, h in pairs]
    bt = [lane(beta_all, c, h, DB_LANE) for c, h in pairs]
    eg = [lane(egam, c, h, DA_LANE) for c, h in pairs]
    decay = [jnp.where(incl, jnp.exp(jnp.where(incl, gcol[p] - gam_rows[h:h + 1, rows[c]], 0.0)), 0.0)
             for p, (c, h) in enumerate(pairs)]
    q = [qn[h][rows[c], :] for c, h in pairs]
    k = [kn[h][rows[c], :] for c, h in pairs]
    v = [qkv[rows[c], 2 * H * K + h * V:2 * H * K + (h + 1) * V] for c, h in pairs]
    n = range(len(pairs))
    kb = [k[p] * bt[p] for p in n]
    k16 = [x.astype(BF16) for x in k]
    kk = [_dot_nt(kb[p].astype(BF16), k16[p]) * decay[p] for p in n]
    qk = [(_dot_nt(q[p].astype(BF16), k16[p]) * decay[p]).astype(BF16) for p in n]
    t_inv = _unit_lower_inverses([jnp.where(strict, x, 0.0) for x in kk], ti, si)
    t16 = [t.astype(BF16) for t in t_inv]
    rhs = [jnp.concatenate([v[p] * bt[p], kb[p] * eg[p]], axis=1).astype(BF16) for p in n]
    sol = [_dot(t16[p], rhs[p]) for p in n]
    q_dec = [(q[p] * eg[p]).astype(BF16) for p in n]
    k_dec = [(k[p] * lane(ekd[c], 0, h, DA_LANE)).astype(BF16) for p, (c, h) in enumerate(pairs)]
    st = [state[h] for h in heads]
    for c in chunks:
        ps = [c * H + h for h in heads]
        st16 = [s.astype(BF16) for s in st]
        v_new = [sol[ps[h]][:, :V] - _dot(sol[ps[h]][:, V:].astype(BF16), st16[h]) for h in heads]
        vn16 = [x.astype(BF16) for x in v_new]
        o = [_dot(q_dec[ps[h]], st16[h]) + _dot(qk[ps[h]], vn16[h]) for h in heads]
        st = [st[h] * elast[c][:, DA_LANE + h:DA_LANE + h + 1] + _dot_tn(k_dec[ps[h]], vn16[h]) for h in heads]
        for h in heads:
            on = o[h] * lax.rsqrt(jnp.mean(o[h] * o[h], axis=-1, keepdims=True) + NORM_EPS) * ng
            o_ref[0, rows[c], h * V:(h + 1) * V] = (on * _silu(og_ref[0, rows[c], h * V:(h + 1) * V])).astype(o_ref.dtype)
    for h in heads:
        state[h] = st[h]


def _gla_gdn_body(qk_ref, v_ref, gog_ref, sm_ref, wg2_ref, bg2_ref, gng_ref,
                  qkv_ref, dog_ref, cw_ref, alog_ref, dtb_ref, dng_ref,
                  yb_ref, yc_ref, gstate, xbuf, dstate):
    @pl.when(pl.program_id(1) == 0)
    def _():
        gstate[...] = jnp.zeros_like(gstate)
        dstate[...] = jnp.zeros_like(dstate)
        xbuf[0:TAIL, :] = jnp.zeros((TAIL, xbuf.shape[1]), F32)

    _gdn_body(qkv_ref, dog_ref, sm_ref, cw_ref, alog_ref, dtb_ref, dng_ref, yc_ref, xbuf, dstate)
    _gla_body(qk_ref, v_ref, gog_ref, sm_ref, wg2_ref, bg2_ref, gng_ref, yb_ref, gstate)


def _gla_gdn(proj, small, wg2p, bg2, gng, cw, alog_p, dtb_p, dng, layer, tt):
    B, S, _ = proj.shape
    C = _tile(S, tt)
    assert C % CHUNK == 0
    QKV = 2 * GDN_HEADS * GDN_DK + GDN_HEADS * GDN_DV
    W = GLA_HEADS * GLA_DV
    full = lambda *shape: _lspec(layer, shape)
    blk = lambda idx: pl.BlockSpec((1, C, W), lambda b, s: (b, s, idx))
    return pl.pallas_call(
        _gla_gdn_body,
        grid=(B, S // C),
        in_specs=[blk(BLK_GQK), blk(BLK_GV), blk(BLK_GOG),
                  pl.BlockSpec((1, C, SMALL_COLS), lambda b, s: (b, s, 0)),
                  full(SMALL_COLS, GLA_HEADS * GLA_DK), full(1, GLA_HEADS * GLA_DK), full(1, GLA_DV),
                  pl.BlockSpec((1, C, QKV), lambda b, s: (b, s, 0)), blk(BLK_DOG),
                  full(GDN_CONV, QKV), full(1, SMALL_COLS), full(1, SMALL_COLS), full(1, GDN_DV)],
        out_specs=[blk(0), blk(0)],
        out_shape=[jax.ShapeDtypeStruct((B, S, W), BF16), jax.ShapeDtypeStruct((B, S, W), BF16)],
        scratch_shapes=[pltpu.VMEM((GLA_HEADS, GLA_DV, GLA_DK), F32),
                        pltpu.VMEM((C + TAIL, QKV), F32), pltpu.VMEM((GDN_HEADS, GDN_DK, GDN_DV), F32)],
        compiler_params=_params("parallel", "arbitrary"),
    )(proj, proj, proj, small, wg2p, bg2, gng, proj, proj, cw, alog_p, dtb_p, dng)


def _merge_body(h_ref, ya_ref, yb_ref, yc_ref, wm_ref, wb_ref, bm_ref, o_ref):
    hx = h_ref[...]
    acc = None
    for n, y_ref in enumerate((ya_ref, yb_ref, yc_ref)):
        gate = _sigmoid(_dot(hx, wm_ref[n]) + bm_ref[n])
        term = gate * _dot(y_ref[...], wb_ref[n])
        acc = term if acc is None else acc + term
    o_ref[...] = acc.astype(o_ref.dtype)


def _merge(hb, ya, yb, yc, wm, wb, bm, layer, tm, tn):
    M, D = hb.shape
    Wb = ya.shape[1]
    tm, tn = _tile(M, tm), _tile(D, tn)
    ybs = pl.BlockSpec((tm, Wb), lambda j, i: (i, 0))
    return pl.pallas_call(
        _merge_body,
        grid=(D // tn, M // tm),
        in_specs=[pl.BlockSpec((tm, D), lambda j, i: (i, 0)), ybs, ybs, ybs,
                  _lspec(layer, (N_BRANCH, D, tn), lambda j, i: (0, 0, j)),
                  _lspec(layer, (N_BRANCH, Wb, tn), lambda j, i: (0, 0, j)),
                  _lspec(layer, (N_BRANCH, 1, tn), lambda j, i: (0, 0, j))],
        out_specs=pl.BlockSpec((tm, tn), lambda j, i: (i, j)),
        out_shape=jax.ShapeDtypeStruct((M, D), BF16),
        compiler_params=_params("parallel", "parallel"),
    )(hb, ya, yb, yc, wm, wb, bm)


def _out_ln_body(m_ref, w_ref, x_ref, g_ref, b_ref, o_ref, ob_ref, y0, y1, *, slab):
    tm = m_ref.shape[0]
    ybufs = (y0, y1)
    for s in range(tm // slab + 1):
        if s < tm // slab:
            ybufs[s % 2][...] = _dot(m_ref[s * slab:(s + 1) * slab, :], w_ref[...])
        if s >= 1:
            rows = slice((s - 1) * slab, s * slab)
            y = DEEPNORM_ALPHA * x_ref[rows, :] + ybufs[(s - 1) % 2][...]
            xn = _layer_norm(y, g_ref[...], b_ref[...])
            o_ref[rows, :] = xn
            ob_ref[rows, :] = xn.astype(BF16)


def _out_ln(merged, w, x, g, b, layer, tm, slab):
    M, D = x.shape
    tm = _tile(M, tm)
    slab = _tile(tm, slab)
    row = pl.BlockSpec((tm, D), lambda i: (i, 0))
    vec = _lspec(layer, (1, D))
    return pl.pallas_call(
        functools.partial(_out_ln_body, slab=slab),
        grid=(M // tm,),
        in_specs=[row, pl.BlockSpec((None, D, D), lambda i: (layer, 0, 0), pipeline_mode=pl.Buffered(1)),
                  row, vec, vec],
        out_specs=[row, row],
        out_shape=[jax.ShapeDtypeStruct((M, D), F32), jax.ShapeDtypeStruct((M, D), BF16)],
        scratch_shapes=[pltpu.VMEM((slab, D), F32), pltpu.VMEM((slab, D), F32)],
        compiler_params=_params("parallel"),
    )(merged, w, x, g, b)


def _ffn_up_body(x_ref, wu_ref, wg_ref, cw_ref, cb_ref, o_ref, tail, g0, g1, u0, u1, *, tiles_per_seq, slab):
    tm = x_ref.shape[0]

    @pl.when(pl.program_id(1) % tiles_per_seq == 0)
    def _():
        tail[...] = jnp.zeros_like(tail)

    cw = cw_ref[...]
    bufs = ((g0, u0), (g1, u1))
    prev_tail = tail[...]
    for s in range(tm // slab + 1):
        if s < tm // slab:
            gb, ub = bufs[s % 2]
            x = x_ref[s * slab:(s + 1) * slab, :]
            ub[...] = _dot(x, wu_ref[...])
            g = _dot(x, wg_ref[...])
            gb[0:TAIL, :] = prev_tail
            gb[TAIL:TAIL + slab, :] = g
            prev_tail = g[slab - TAIL:slab, :]
        if s >= 1:
            gb, ub = bufs[(s - 1) % 2]
            gt = cb_ref[...] + cw[2:3, :] * gb[TAIL:TAIL + slab, :]
            for j in range(FFN_CONV - 1):
                gt = gt + cw[j:j + 1, :] * gb[pl.ds(TAIL - (FFN_CONV - 1) + j, slab), :]
            o_ref[(s - 1) * slab:s * slab, :] = (_silu(gt) * ub[...]).astype(o_ref.dtype)
    tail[...] = prev_tail


def _ffn_up(xb, wu, wg, cw, cb, layer, seq, tm, tf, slab):
    M, D = xb.shape
    F = wu.shape[2]
    tm, tf = _tile(seq, tm), _tile(F, tf)
    slab = _tile(tm, slab)
    wspec = _lspec(layer, (D, tf), lambda j, i: (0, j))
    return pl.pallas_call(
        functools.partial(_ffn_up_body, tiles_per_seq=seq // tm, slab=slab),
        grid=(F // tf, M // tm),
        in_specs=[pl.BlockSpec((tm, D), lambda j, i: (i, 0)), wspec, wspec,
                  _lspec(layer, (FFN_CONV, tf), lambda j, i: (0, j)),
                  _lspec(layer, (1, tf), lambda j, i: (0, j))],
        out_specs=pl.BlockSpec((tm, tf), lambda j, i: (i, j)),
        out_shape=jax.ShapeDtypeStruct((M, F), BF16),
        scratch_shapes=[pltpu.VMEM((TAIL, tf), F32),
                        pltpu.VMEM((slab + TAIL, tf), F32), pltpu.VMEM((slab + TAIL, tf), F32),
                        pltpu.VMEM((slab, tf), F32), pltpu.VMEM((slab, tf), F32)],
        compiler_params=_params("arbitrary", "arbitrary"),
    )(xb, wu, wg, cw, cb)


def _ple_body(p_ref, x_ref, wp_ref, wg_ref, o_ref):
    proj = _dot(p_ref[...].astype(BF16), wp_ref[...])
    o_ref[...] = proj * _sigmoid(_dot(x_ref[...], wg_ref[...]))


def _ple(p, xb, wp, wg, layer, tm, tn):
    M, D = xb.shape
    P = p.shape[2]
    tm, tn = _tile(M, tm), _tile(D, tn)
    return pl.pallas_call(
        _ple_body,
        grid=(D // tn, M // tm),
        in_specs=[_lspec(layer, (tm, P), lambda j, i: (i, 0)), pl.BlockSpec((tm, D), lambda j, i: (i, 0)),
                  _lspec(layer, (P, tn), lambda j, i: (0, j)), _lspec(layer, (D, tn), lambda j, i: (0, j))],
        out_specs=pl.BlockSpec((tm, tn), lambda j, i: (i, j)),
        out_shape=jax.ShapeDtypeStruct((M, D), F32),
        compiler_params=_params("parallel", "parallel"),
    )(p, xb, wp, wg)


def _down_ln_body(a_ref, w_ref, ple_ref, x_ref, g_ref, b_ref, o_ref, ob_ref):
    y = DEEPNORM_ALPHA * x_ref[...] + _dot(a_ref[...], w_ref[...]) + ple_ref[...]
    xn = _layer_norm(y, g_ref[...], b_ref[...])
    o_ref[...] = xn
    ob_ref[...] = xn.astype(BF16)


def _down_ln(act, w, ple, x, g, b, layer, tm):
    M, D = x.shape
    F = act.shape[1]
    tm = _tile(M, tm)
    row = pl.BlockSpec((tm, D), lambda i: (i, 0))
    vec = _lspec(layer, (1, D))
    return pl.pallas_call(
        _down_ln_body,
        grid=(M // tm,),
        in_specs=[pl.BlockSpec((tm, F), lambda i: (i, 0)),
                  pl.BlockSpec((None, F, D), lambda i: (layer, 0, 0), pipeline_mode=pl.Buffered(1)),
                  row, row, vec, vec],
        out_specs=[row, row],
        out_shape=[jax.ShapeDtypeStruct((M, D), F32), jax.ShapeDtypeStruct((M, D), BF16)],
        compiler_params=_params("parallel"),
    )(act, w, ple, x, g, b)


def _block_diag(w):
    L, G, gs, _ = w.shape
    per = LRU_BD // gs
    w5 = w.reshape(L, G // per, per, gs, gs)
    eye = jnp.eye(per, dtype=w.dtype)
    return jnp.einsum('lkaij,ab->lkaibj', w5, eye).reshape(L, G // per, LRU_BD, LRU_BD)


def kernel(x, p, w_in, lru_conv_w, lru_conv_b, lru_wr, lru_br, lru_wi, lru_bi, lru_lambda, gla_wg2, gla_bg2, gla_norm_g, gdn_conv_w, gdn_a_log, gdn_dt_bias, gdn_norm_g, w_branch, w_merge, b_merge, w_out, ln1_g, ln1_b, ffn_w_up, ffn_w_gate, ffn_conv_w, ffn_conv_b, ffn_w_down, ple_w_proj, ple_w_gate, ln2_g, ln2_b):
    B, S, D = x.shape
    L = w_in.shape[0]
    M = B * S

    o_lx, o_glr, o_gog, o_dq, o_da, o_dog, o_end = 0, 4096, 4112, 5136, 8208, 8224, 9248
    w_main = jnp.concatenate([w_in[:, :, o_dq:o_da], w_in[:, :, o_lx:o_glr], w_in[:, :, o_gog:o_dq],
                              w_in[:, :, o_dog:o_end]], axis=2).astype(BF16)
    w_small = jnp.concatenate([w_in[:, :, o_glr:o_gog], w_in[:, :, o_da:o_dog],
                               jnp.zeros((L, D, SMALL_COLS - GLA_RANK - 2 * GDN_HEADS), w_in.dtype)], axis=2).astype(BF16)
    nk = LRU_WIDTH // LRU_BD
    lru_wbd = jnp.concatenate([_block_diag(lru_wr), _block_diag(lru_wi)], axis=3).astype(BF16)
    lru_bri = jnp.concatenate([lru_br.reshape(L, nk, 1, LRU_BD), lru_bi.reshape(L, nk, 1, LRU_BD)], axis=3)
    wg2p = jnp.concatenate([gla_wg2, jnp.zeros((L, SMALL_COLS - GLA_RANK, gla_wg2.shape[2]), gla_wg2.dtype)],
                           axis=1).astype(BF16)
    lane_pad = lambda v: jnp.pad(v, ((0, 0), (DA_LANE, SMALL_COLS - DA_LANE - GDN_HEADS)))[:, None, :]
    alog_p, dtb_p = lane_pad(gdn_a_log), lane_pad(gdn_dt_bias)
    wm16, wb16, wo16 = w_merge.astype(BF16), w_branch.astype(BF16), w_out.astype(BF16)
    wu16, wgt16, wd16 = ffn_w_up.astype(BF16), ffn_w_gate.astype(BF16), ffn_w_down.astype(BF16)
    wp16, wpg16 = ple_w_proj.astype(BF16), ple_w_gate.astype(BF16)
    rows = lambda v: v[:, None, :]
    pf = p.reshape(L, M, PLE_DIM)

    xf = x.reshape(M, D)
    xb = xf.astype(BF16)
    for i in range(L):
        proj = _matmul(xb, w_main, i, F32, 1024, 1024).reshape(B, S, MAIN_COLS)
        small = _matmul(xb, w_small, i, F32, 2048, SMALL_COLS).reshape(B, S, SMALL_COLS)
        ya = _lru(proj, lru_conv_w, rows(lru_conv_b), lru_wbd, lru_bri, rows(lru_lambda), i, 256)
        yb, yc = _gla_gdn(proj, small, wg2p, rows(gla_bg2), rows(gla_norm_g),
                          gdn_conv_w, alog_p, dtb_p, rows(gdn_norm_g), i, 256)
        merged = _merge(xb, ya.reshape(M, -1), yb.reshape(M, -1), yc.reshape(M, -1),
                        wm16, wb16, b_merge[:, :, None, :], i, 512, 512)
        xf, xb = _out_ln(merged, wo16, xf, rows(ln1_g), rows(ln1_b), i, 512, 128)
        act = _ffn_up(xb, wu16, wgt16, ffn_conv_w, rows(ffn_conv_b), i, S, 2048, 512, 256)
        ple = _ple(pf, xb, wp16, wpg16, i, 1024, 1024)
        xf, xb = _down_ln(act, wd16, ple, xf, rows(ln2_g), rows(ln2_b), i, 256)
    return xf.reshape(B, S, D)
```

```python
import functools

import jax
import jax.numpy as jnp
from jax import lax
from jax.experimental import pallas as pl
from jax.experimental.pallas import tpu as pltpu

F32 = jnp.float32
BF16 = jnp.bfloat16

D_MODEL = 2048
DEPTH = 4
CHUNK = 64
PLE_DIM = 256
LRU_WIDTH = 1024
LRU_GROUPS = 16
LRU_CONV = 4
LRU_C = 8.0
GLA_HEADS = 4
GLA_DK = 128
GLA_DV = 256
GLA_RANK = 16
GLA_GATE_NORM = 16.0
GDN_HEADS = 8
GDN_DK = 128
GDN_DV = 128
GDN_CONV = 4
N_BRANCH = 3
D_FF = 5632
FFN_CONV = 3
LN_EPS = 1e-5
NORM_EPS = 1e-6
DEEPNORM_ALPHA = (2 * DEPTH) ** 0.25

MAIN_COLS = 9216
SMALL_COLS = 128
BLK_LRU_X, BLK_LRU_GATE, BLK_GQK, BLK_GV, BLK_GOG, BLK_DOG = 3, 4, 5, 6, 7, 8
GDN_TILES = 3
CONV_TILES = 4
assert LRU_CONV == GDN_CONV
PROJ_CONV = LRU_CONV
DA_LANE = 16
DB_LANE = 24
LRU_BD = 256
SUBLANES = 8
TAIL = SUBLANES

VMEM_LIMIT = 56 * 1024 * 1024


def _dot(a, b):
    return jnp.dot(a, b, preferred_element_type=F32)


def _dot_nt(a, b):
    return lax.dot_general(a, b, (((1,), (1,)), ((), ())), preferred_element_type=F32)


def _dot_tn(a, b):
    return lax.dot_general(a, b, (((0,), (0,)), ((), ())), preferred_element_type=F32)


def _split2(x):
    hi = x.astype(BF16)
    lo = (x - hi.astype(F32)).astype(BF16)
    return hi, lo


def _dot_hp(a, b):
    ah, al = _split2(a)
    bh, bl = _split2(b)
    return _dot(ah, bh) + (_dot(ah, bl) + _dot(al, bh))


def _split3(x):
    x1 = x.astype(BF16)
    r = x - x1.astype(F32)
    x2 = r.astype(BF16)
    x3 = (r - x2.astype(F32)).astype(BF16)
    return x1, x2, x3


def _softplus(x):
    return jnp.maximum(x, 0.0) + jnp.log1p(jnp.exp(-jnp.abs(x)))


def _sigmoid(x):
    return 1.0 / (1.0 + jnp.exp(-x))


def _silu(x):
    return x * _sigmoid(x)


def _layer_norm(y, g, b):
    mu = jnp.mean(y, axis=-1, keepdims=True)
    d = y - mu
    var = jnp.mean(d * d, axis=-1, keepdims=True)
    return d * lax.rsqrt(var + LN_EPS) * g + b


def _tile(dim, pref):
    t = min(dim, pref)
    assert dim % t == 0, (dim, pref)
    return t


def _params(*sem):
    return pltpu.CompilerParams(dimension_semantics=sem, vmem_limit_bytes=VMEM_LIMIT)


def _mm_body(x_ref, w_ref, o_ref):
    o_ref[...] = _dot(x_ref[...], w_ref[...]).astype(o_ref.dtype)


def _lspec(layer, shape, index_map=None):
    if index_map is None:
        index_map = lambda *g: (0,) * len(shape)
    return pl.BlockSpec((None,) + tuple(shape), lambda *g: (layer,) + tuple(index_map(*g)))


def _matmul(x, w, layer, out_dtype, tm, tn):
    M, K = x.shape
    N = w.shape[2]
    tm, tn = _tile(M, tm), _tile(N, tn)
    return pl.pallas_call(
        _mm_body,
        grid=(N // tn, M // tm),
        in_specs=[pl.BlockSpec((tm, K), lambda j, i: (i, 0)),
                  _lspec(layer, (K, tn), lambda j, i: (0, j))],
        out_specs=pl.BlockSpec((tm, tn), lambda j, i: (i, j)),
        out_shape=jax.ShapeDtypeStruct((M, N), out_dtype),
        compiler_params=_params("parallel", "parallel"),
    )(x, w)


def _proj_body(x_ref, w_ref, cw_ref, cb_ref, o_ref, g0, g1, *, slab):
    j = pl.program_id(0)
    tm = x_ref.shape[0]
    nslab = tm // slab

    def conv_path(act):
        cw = cw_ref[...]
        bufs = (g0, g1)
        prev_tail = jnp.zeros((TAIL, g0.shape[1]), F32)
        for s in range(nslab + 1):
            if s < nslab:
                gb = bufs[s % 2]
                g = _dot(x_ref[s * slab:(s + 1) * slab, :], w_ref[...])
                gb[0:TAIL, :] = prev_tail
                gb[TAIL:TAIL + slab, :] = g
                prev_tail = g[slab - TAIL:slab, :]
            if s >= 1:
                gb = bufs[(s - 1) % 2]
                y = cb_ref[...] + cw[PROJ_CONV - 1:PROJ_CONV, :] * gb[TAIL:TAIL + slab, :]
                for t in range(PROJ_CONV - 1):
                    y = y + cw[t:t + 1, :] * gb[pl.ds(TAIL - (PROJ_CONV - 1) + t, slab), :]
                o_ref[(s - 1) * slab:s * slab, :] = act(y)

    @pl.when(j < GDN_TILES)
    def _():
        conv_path(_silu)

    @pl.when((j >= GDN_TILES) & (j < CONV_TILES))
    def _():
        conv_path(lambda y: y)

    @pl.when(j >= CONV_TILES)
    def _():
        o_ref[...] = _dot(x_ref[...], w_ref[...])


def _proj(x, w, cw, cb, layer, seq, tn, slab):
    M, K = x.shape
    N = w.shape[2]
    assert N % tn == 0 and tn == 1024 and seq % slab == 0
    last = CONV_TILES - 1
    return pl.pallas_call(
        functools.partial(_proj_body, slab=slab),
        grid=(N // tn, M // seq),
        in_specs=[pl.BlockSpec((seq, K), lambda j, i: (i, 0)),
                  _lspec(layer, (K, tn), lambda j, i: (0, j)),
                  _lspec(layer, (PROJ_CONV, tn), lambda j, i: (0, jnp.minimum(j, last))),
                  _lspec(layer, (1, tn), lambda j, i: (0, jnp.minimum(j, last)))],
        out_specs=pl.BlockSpec((seq, tn), lambda j, i: (i, j)),
        out_shape=jax.ShapeDtypeStruct((M, N), F32),
        scratch_shapes=[pltpu.VMEM((slab + TAIL, tn), F32), pltpu.VMEM((slab + TAIL, tn), F32)],
        compiler_params=_params("parallel", "parallel"),
    )(x, w, cw, cb)


def _lru_body(x_ref, gate_ref, wbd_ref, bri_ref, lam_ref, o_ref, hcarry):
    tc = x_ref.shape[1]

    @pl.when(pl.program_id(1) == 0)
    def _():
        hcarry[...] = jnp.zeros_like(hcarry)

    xa = x_ref[0]
    lam = lam_ref[...]
    neg_c_sp = -LRU_C * _softplus(-lam)
    sub = lax.broadcasted_iota(jnp.int32, (tc // SUBLANES, SUBLANES, LRU_BD), 1)
    for k in range(LRU_WIDTH // LRU_BD):
        cs = slice(k * LRU_BD, (k + 1) * LRU_BD)
        xk = xa[:, cs]
        ri = _dot(xk.astype(BF16), wbd_ref[k]) + bri_ref[k]
        r = _sigmoid(ri[:, :LRU_BD])
        ig = _sigmoid(ri[:, LRU_BD:])
        log_a = r * neg_c_sp[:, cs]
        a = jnp.exp(log_a)
        u = jnp.sqrt(-jnp.tanh(log_a) * (a * a + 1.0)) * (ig * xk)
        a = a.reshape(tc // SUBLANES, SUBLANES, LRU_BD)
        u = u.reshape(tc // SUBLANES, SUBLANES, LRU_BD)
        d = 1
        while d < SUBLANES:
            keep = sub >= d
            a_s = jnp.where(keep, pltpu.roll(a, d, 1), 1.0)
            u_s = jnp.where(keep, pltpu.roll(u, d, 1), 0.0)
            u = a * u_s + u
            a = a * a_s
            d *= 2
        hc = hcarry[:, cs]
        hs = []
        for g in range(tc // SUBLANES):
            hg = u[g] + a[g] * hc
            hs.append(hg)
            hc = hg[SUBLANES - 1:SUBLANES, :]
        h = jnp.concatenate(hs, axis=0)
        hcarry[:, cs] = hc
        o_ref[0, :, cs] = (h * jax.nn.gelu(gate_ref[0, :, cs], approximate=True)).astype(o_ref.dtype)


def _lru(proj, wbd, bri, lam, layer, tc):
    B, S, _ = proj.shape
    tc = _tile(S, tc)
    W = LRU_WIDTH
    nk = W // LRU_BD
    full = lambda *shape: _lspec(layer, shape)
    return pl.pallas_call(
        _lru_body,
        grid=(B, S // tc),
        in_specs=[pl.BlockSpec((1, tc, W), lambda b, s: (b, s, BLK_LRU_X)),
                  pl.BlockSpec((1, tc, W), lambda b, s: (b, s, BLK_LRU_GATE)),
                  full(nk, LRU_BD, 2 * LRU_BD), full(nk, 1, 2 * LRU_BD), full(1, W)],
        out_specs=pl.BlockSpec((1, tc, W), lambda b, s: (b, s, 0)),
        out_shape=jax.ShapeDtypeStruct((B, S, W), BF16),
        scratch_shapes=[pltpu.VMEM((1, W), F32)],
        compiler_params=_params("parallel", "arbitrary"),
    )(proj, proj, wbd, bri, lam)


def _gla_body(qk_ref, v_ref, og_ref, sm_ref, wg2_ref, bg2_ref, ng_ref, o_ref, state):
    H, K, V, C = GLA_HEADS, GLA_DK, GLA_DV, CHUNK
    T = qk_ref.shape[1]
    chunks = range(T // C)
    heads = range(H)
    z = _dot(sm_ref[0].astype(BF16), wg2_ref[...]) + bg2_ref[...]
    fg = (jnp.minimum(z, 0.0) - jnp.log1p(jnp.exp(-jnp.abs(z)))) * (1.0 / GLA_GATE_NORM)
    tt = lax.broadcasted_iota(jnp.int32, (T, T), 0)
    ss = lax.broadcasted_iota(jnp.int32, (T, T), 1)
    tri = jnp.where((tt >= ss) & (tt // C == ss // C), 1.0, 0.0).astype(BF16)
    f1, f2, f3 = _split3(fg)
    bcum = _dot(tri, f1) + (_dot(tri, f2) + _dot(tri, f3))
    causal = lax.broadcasted_iota(jnp.int32, (C, C), 0) >= lax.broadcasted_iota(jnp.int32, (C, C), 1)
    ng = ng_ref[...]
    qk = qk_ref[0]
    q_all = qk[:, :H * K] * (K ** -0.5)
    k_all = qk[:, H * K:]
    q_dec_all = (q_all * jnp.exp(bcum)).astype(BF16)
    k_neg_all = (k_all * jnp.exp(-bcum)).astype(BF16)
    rows = [slice(c * C, (c + 1) * C) for c in chunks]
    bl = [bcum[(c + 1) * C - 1:(c + 1) * C, :] for c in chunks]
    k_dec = [(k_all[rows[c], :] * jnp.exp(bl[c] - bcum[rows[c], :])).astype(BF16) for c in chunks]
    ebl = [jnp.exp(bl[c]) for c in chunks]
    ks = [slice(h * K, (h + 1) * K) for h in heads]
    vs = [slice(h * V, (h + 1) * V) for h in heads]
    v = [[v_ref[0, rows[c], vs[h]].astype(BF16) for h in heads] for c in chunks]
    scores = [[jnp.where(causal, _dot_nt(q_dec_all[rows[c], ks[h]], k_neg_all[rows[c], ks[h]]), 0.0).astype(BF16)
               for h in heads] for c in chunks]
    intra = [[_dot(scores[c][h], v[c][h]) for h in heads] for c in chunks]
    st = [state[h] for h in heads]
    for c in chunks:
        o = [intra[c][h] + _dot_nt(q_dec_all[rows[c], ks[h]], st[h].astype(BF16)) for h in heads]
        st = [st[h] * ebl[c][:, ks[h]] + _dot_tn(v[c][h], k_dec[c][:, ks[h]]) for h in heads]
        for h in heads:
            on = o[h] * lax.rsqrt(jnp.mean(o[h] * o[h], axis=-1, keepdims=True) + NORM_EPS) * ng
            o_ref[0, rows[c], vs[h]] = (on * _silu(og_ref[0, rows[c], vs[h]])).astype(o_ref.dtype)
    for h in heads:
        state[h] = st[h]


def _unit_lower_inverses(n_list, ti, si):
    C = n_list[0].shape[0]
    eye = jnp.where(ti == si, 1.0, 0.0)
    pair = ti // 2 == si // 2
    ts = [eye - jnp.where(pair, n, 0.0) for n in n_list]
    d = 2
    while d < C:
        m = (ti // (2 * d) == si // (2 * d)) & (ti % (2 * d) >= d) & (si % (2 * d) < d)
        cds = [jnp.where(m, n, 0.0).astype(BF16) for n in n_list]
        tbs = [t.astype(BF16) for t in ts]
        tcs = [_dot(tb, cd).astype(BF16) for tb, cd in zip(tbs, cds)]
        ts = [t - _dot(tc, tb) for t, tc, tb in zip(ts, tcs, tbs)]
        d *= 2
    return ts


def _gdn_body(qkv_ref, og_ref, sm_ref, alog_ref, dtb_ref, ng_ref, o_ref, state):
    H, K, V, C = GDN_HEADS, GDN_DK, GDN_DV, CHUNK
    T = qkv_ref.shape[1]
    chunks = range(T // C)
    heads = range(H)
    qkv = qkv_ref[0]
    sm = sm_ref[0]
    beta_all = _sigmoid(sm)
    g_all = -jnp.exp(alog_ref[...]) * _softplus(sm + dtb_ref[...])
    tt = lax.broadcasted_iota(jnp.int32, (T, T), 0)
    ss = lax.broadcasted_iota(jnp.int32, (T, T), 1)
    tri = jnp.where((tt >= ss) & (tt // C == ss // C), 1.0, 0.0).astype(BF16)
    g1, g2, g3 = _split3(g_all)
    gam = _dot(tri, g1) + (_dot(tri, g2) + _dot(tri, g3))
    sel = jnp.where(lax.broadcasted_iota(jnp.int32, (SUBLANES, SMALL_COLS), 1)
                    == lax.broadcasted_iota(jnp.int32, (SUBLANES, SMALL_COLS), 0) + DA_LANE, 1.0, 0.0).astype(BF16)
    m1, m2, m3 = _split3(gam)
    gam_rows = _dot_nt(sel, m1) + (_dot_nt(sel, m2) + _dot_nt(sel, m3))
    egam = jnp.exp(gam)
    ti = lax.broadcasted_iota(jnp.int32, (C, C), 0)
    si = lax.broadcasted_iota(jnp.int32, (C, C), 1)
    incl = ti >= si
    strict = ti > si
    ng = ng_ref[...]
    rows = [slice(c * C, (c + 1) * C) for c in chunks]
    gl_row = [gam[(c + 1) * C - 1:(c + 1) * C, :] for c in chunks]
    ekd = [jnp.exp(gl_row[c] - gam[rows[c], :]) for c in chunks]
    elast = [jnp.exp(gl_row[c]) for c in chunks]
    qn = [qkv[:, h * K:(h + 1) * K] for h in heads]
    kn = [qkv[:, (H + h) * K:(H + h + 1) * K] for h in heads]
    qn = [x * (lax.rsqrt(jnp.sum(x * x, axis=-1, keepdims=True) + NORM_EPS) * (K ** -0.5)) for x in qn]
    kn = [x * lax.rsqrt(jnp.sum(x * x, axis=-1, keepdims=True) + NORM_EPS) for x in kn]

    pairs = [(c, h) for c in chunks for h in heads]
    lane = lambda x, c, h, base: x[rows[c], base + h:base + h + 1]
    gcol = [lane(gam, c, h, DA_LANE) for c, h in pairs]
    bt = [lane(beta_all, c, h, DB_LANE) for c, h in pairs]
    eg = [lane(egam, c, h, DA_LANE) for c, h in pairs]
    decay = [jnp.where(incl, jnp.exp(jnp.where(incl, gcol[p] - gam_rows[h:h + 1, rows[c]], 0.0)), 0.0)
             for p, (c, h) in enumerate(pairs)]
    q = [qn[h][rows[c], :] for c, h in pairs]
    k = [kn[h][rows[c], :] for c, h in pairs]
    v = [qkv[rows[c], 2 * H * K + h * V:2 * H * K + (h + 1) * V] for c, h in pairs]
    n = range(len(pairs))
    kb = [k[p] * bt[p] for p in n]
    k16 = [x.astype(BF16) for x in k]
    kk = [_dot_nt(kb[p].astype(BF16), k16[p]) * decay[p] for p in n]
    qk = [(_dot_nt(q[p].astype(BF16), k16[p]) * decay[p]).astype(BF16) for p in n]
    t_inv = _unit_lower_inverses([jnp.where(strict, x, 0.0) for x in kk], ti, si)
    t16 = [t.astype(BF16) for t in t_inv]
    rhs = [jnp.concatenate([v[p] * bt[p], kb[p] * eg[p]], axis=1).astype(BF16) for p in n]
    sol = [_dot(t16[p], rhs[p]) for p in n]
    q_dec = [(q[p] * eg[p]).astype(BF16) for p in n]
    k_dec = [(k[p] * lane(ekd[c], 0, h, DA_LANE)).astype(BF16) for p, (c, h) in enumerate(pairs)]
    st = [state[h] for h in heads]
    for c in chunks:
        ps = [c * H + h for h in heads]
        st16 = [s.astype(BF16) for s in st]
        v_new = [sol[ps[h]][:, :V] - _dot(sol[ps[h]][:, V:].astype(BF16), st16[h]) for h in heads]
        vn16 = [x.astype(BF16) for x in v_new]
        o = [_dot(q_dec[ps[h]], st16[h]) + _dot(qk[ps[h]], vn16[h]) for h in heads]
        st = [st[h] * elast[c][:, DA_LANE + h:DA_LANE + h + 1] + _dot_tn(k_dec[ps[h]], vn16[h]) for h in heads]
        for h in heads:
            on = o[h] * lax.rsqrt(jnp.mean(o[h] * o[h], axis=-1, keepdims=True) + NORM_EPS) * ng
            o_ref[0, rows[c], h * V:(h + 1) * V] = (on * _silu(og_ref[0, rows[c], h * V:(h + 1) * V])).astype(o_ref.dtype)
    for h in heads:
        state[h] = st[h]


def _gla_gdn_body(qk_ref, v_ref, gog_ref, sm_ref, wg2_ref, bg2_ref, gng_ref,
                  qkv_ref, dog_ref, alog_ref, dtb_ref, dng_ref,
                  yb_ref, yc_ref, gstate, dstate):
    @pl.when(pl.program_id(1) == 0)
    def _():
        gstate[...] = jnp.zeros_like(gstate)
        dstate[...] = jnp.zeros_like(dstate)

    _gdn_body(qkv_ref, dog_ref, sm_ref, alog_ref, dtb_ref, dng_ref, yc_ref, dstate)
    _gla_body(qk_ref, v_ref, gog_ref, sm_ref, wg2_ref, bg2_ref, gng_ref, yb_ref, gstate)


def _gla_gdn(proj, small, wg2p, bg2, gng, alog_p, dtb_p, dng, layer, tt):
    B, S, _ = proj.shape
    C = _tile(S, tt)
    assert C % CHUNK == 0
    QKV = 2 * GDN_HEADS * GDN_DK + GDN_HEADS * GDN_DV
    W = GLA_HEADS * GLA_DV
    full = lambda *shape: _lspec(layer, shape)
    blk = lambda idx: pl.BlockSpec((1, C, W), lambda b, s: (b, s, idx))
    return pl.pallas_call(
        _gla_gdn_body,
        grid=(B, S // C),
        in_specs=[blk(BLK_GQK), blk(BLK_GV), blk(BLK_GOG),
                  pl.BlockSpec((1, C, SMALL_COLS), lambda b, s: (b, s, 0)),
                  full(SMALL_COLS, GLA_HEADS * GLA_DK), full(1, GLA_HEADS * GLA_DK), full(1, GLA_DV),
                  pl.BlockSpec((1, C, QKV), lambda b, s: (b, s, 0)), blk(BLK_DOG),
                  full(1, SMALL_COLS), full(1, SMALL_COLS), full(1, GDN_DV)],
        out_specs=[blk(0), blk(0)],
        out_shape=[jax.ShapeDtypeStruct((B, S, W), BF16), jax.ShapeDtypeStruct((B, S, W), BF16)],
        scratch_shapes=[pltpu.VMEM((GLA_HEADS, GLA_DV, GLA_DK), F32),
                        pltpu.VMEM((GDN_HEADS, GDN_DK, GDN_DV), F32)],
        compiler_params=_params("parallel", "arbitrary"),
    )(proj, proj, proj, small, wg2p, bg2, gng, proj, proj, alog_p, dtb_p, dng)


def _merge_body(h_ref, ya_ref, yb_ref, yc_ref, wm_ref, wb_ref, bm_ref, o_ref):
    hx = h_ref[...]
    acc = None
    for n, y_ref in enumerate((ya_ref, yb_ref, yc_ref)):
        gate = _sigmoid(_dot(hx, wm_ref[n]) + bm_ref[n])
        term = gate * _dot(y_ref[...], wb_ref[n])
        acc = term if acc is None else acc + term
    o_ref[...] = acc.astype(o_ref.dtype)


def _merge(hb, ya, yb, yc, wm, wb, bm, layer, tm, tn):
    M, D = hb.shape
    Wb = ya.shape[1]
    tm, tn = _tile(M, tm), _tile(D, tn)
    ybs = pl.BlockSpec((tm, Wb), lambda j, i: (i, 0))
    return pl.pallas_call(
        _merge_body,
        grid=(D // tn, M // tm),
        in_specs=[pl.BlockSpec((tm, D), lambda j, i: (i, 0)), ybs, ybs, ybs,
                  _lspec(layer, (N_BRANCH, D, tn), lambda j, i: (0, 0, j)),
                  _lspec(layer, (N_BRANCH, Wb, tn), lambda j, i: (0, 0, j)),
                  _lspec(layer, (N_BRANCH, 1, tn), lambda j, i: (0, 0, j))],
        out_specs=pl.BlockSpec((tm, tn), lambda j, i: (i, j)),
        out_shape=jax.ShapeDtypeStruct((M, D), BF16),
        compiler_params=_params("parallel", "parallel"),
    )(hb, ya, yb, yc, wm, wb, bm)


def _out_ln_body(m_ref, w_ref, x_ref, g_ref, b_ref, o_ref, ob_ref, y0, y1, *, slab):
    tm = m_ref.shape[0]
    ybufs = (y0, y1)
    for s in range(tm // slab + 1):
        if s < tm // slab:
            ybufs[s % 2][...] = _dot(m_ref[s * slab:(s + 1) * slab, :], w_ref[...])
        if s >= 1:
            rows = slice((s - 1) * slab, s * slab)
            y = DEEPNORM_ALPHA * x_ref[rows, :] + ybufs[(s - 1) % 2][...]
            xn = _layer_norm(y, g_ref[...], b_ref[...])
            o_ref[rows, :] = xn
            ob_ref[rows, :] = xn.astype(BF16)


def _out_ln(merged, w, x, g, b, layer, tm, slab):
    M, D = x.shape
    tm = _tile(M, tm)
    slab = _tile(tm, slab)
    row = pl.BlockSpec((tm, D), lambda i: (i, 0))
    vec = _lspec(layer, (1, D))
    return pl.pallas_call(
        functools.partial(_out_ln_body, slab=slab),
        grid=(M // tm,),
        in_specs=[row, pl.BlockSpec((None, D, D), lambda i: (layer, 0, 0), pipeline_mode=pl.Buffered(1)),
                  row, vec, vec],
        out_specs=[row, row],
        out_shape=[jax.ShapeDtypeStruct((M, D), F32), jax.ShapeDtypeStruct((M, D), BF16)],
        scratch_shapes=[pltpu.VMEM((slab, D), F32), pltpu.VMEM((slab, D), F32)],
        compiler_params=_params("parallel"),
    )(merged, w, x, g, b)


def _ffn_up_body(x_ref, wu_ref, wg_ref, cw_ref, cb_ref, o_ref, tail, g0, g1, u0, u1, *, tiles_per_seq, slab):
    tm = x_ref.shape[0]

    @pl.when(pl.program_id(1) % tiles_per_seq == 0)
    def _():
        tail[...] = jnp.zeros_like(tail)

    cw = cw_ref[...]
    bufs = ((g0, u0), (g1, u1))
    prev_tail = tail[...]
    for s in range(tm // slab + 1):
        if s < tm // slab:
            gb, ub = bufs[s % 2]
            x = x_ref[s * slab:(s + 1) * slab, :]
            ub[...] = _dot(x, wu_ref[...])
            g = _dot(x, wg_ref[...])
            gb[0:TAIL, :] = prev_tail
            gb[TAIL:TAIL + slab, :] = g
            prev_tail = g[slab - TAIL:slab, :]
        if s >= 1:
            gb, ub = bufs[(s - 1) % 2]
            gt = cb_ref[...] + cw[2:3, :] * gb[TAIL:TAIL + slab, :]
            for j in range(FFN_CONV - 1):
                gt = gt + cw[j:j + 1, :] * gb[pl.ds(TAIL - (FFN_CONV - 1) + j, slab), :]
            o_ref[(s - 1) * slab:s * slab, :] = (_silu(gt) * ub[...]).astype(o_ref.dtype)
    tail[...] = prev_tail


def _ffn_up(xb, wu, wg, cw, cb, layer, seq, tm, tf, slab):
    M, D = xb.shape
    F = wu.shape[2]
    tm, tf = _tile(seq, tm), _tile(F, tf)
    slab = _tile(tm, slab)
    wspec = _lspec(layer, (D, tf), lambda j, i: (0, j))
    return pl.pallas_call(
        functools.partial(_ffn_up_body, tiles_per_seq=seq // tm, slab=slab),
        grid=(F // tf, M // tm),
        in_specs=[pl.BlockSpec((tm, D), lambda j, i: (i, 0)), wspec, wspec,
                  _lspec(layer, (FFN_CONV, tf), lambda j, i: (0, j)),
                  _lspec(layer, (1, tf), lambda j, i: (0, j))],
        out_specs=pl.BlockSpec((tm, tf), lambda j, i: (i, j)),
        out_shape=jax.ShapeDtypeStruct((M, F), BF16),
        scratch_shapes=[pltpu.VMEM((TAIL, tf), F32),
                        pltpu.VMEM((slab + TAIL, tf), F32), pltpu.VMEM((slab + TAIL, tf), F32),
                        pltpu.VMEM((slab, tf), F32), pltpu.VMEM((slab, tf), F32)],
        compiler_params=_params("arbitrary", "arbitrary"),
    )(xb, wu, wg, cw, cb)


def _ple_body(p_ref, x_ref, wp_ref, wg_ref, o_ref):
    proj = _dot(p_ref[...].astype(BF16), wp_ref[...])
    o_ref[...] = proj * _sigmoid(_dot(x_ref[...], wg_ref[...]))


def _ple(p, xb, wp, wg, layer, tm, tn):
    M, D = xb.shape
    P = p.shape[2]
    tm, tn = _tile(M, tm), _tile(D, tn)
    return pl.pallas_call(
        _ple_body,
        grid=(D // tn, M // tm),
        in_specs=[_lspec(layer, (tm, P), lambda j, i: (i, 0)), pl.BlockSpec((tm, D), lambda j, i: (i, 0)),
                  _lspec(layer, (P, tn), lambda j, i: (0, j)), _lspec(layer, (D, tn), lambda j, i: (0, j))],
        out_specs=pl.BlockSpec((tm, tn), lambda j, i: (i, j)),
        out_shape=jax.ShapeDtypeStruct((M, D), F32),
        compiler_params=_params("parallel", "parallel"),
    )(p, xb, wp, wg)


def _down_ln_body(a_ref, w_ref, ple_ref, x_ref, g_ref, b_ref, o_ref, ob_ref):
    y = DEEPNORM_ALPHA * x_ref[...] + _dot(a_ref[...], w_ref[...]) + ple_ref[...]
    xn = _layer_norm(y, g_ref[...], b_ref[...])
    o_ref[...] = xn
    ob_ref[...] = xn.astype(BF16)


def _down_ln(act, w, ple, x, g, b, layer, tm):
    M, D = x.shape
    F = act.shape[1]
    tm = _tile(M, tm)
    row = pl.BlockSpec((tm, D), lambda i: (i, 0))
    vec = _lspec(layer, (1, D))
    return pl.pallas_call(
        _down_ln_body,
        grid=(M // tm,),
        in_specs=[pl.BlockSpec((tm, F), lambda i: (i, 0)),
                  pl.BlockSpec((None, F, D), lambda i: (layer, 0, 0), pipeline_mode=pl.Buffered(1)),
                  row, row, vec, vec],
        out_specs=[row, row],
        out_shape=[jax.ShapeDtypeStruct((M, D), F32), jax.ShapeDtypeStruct((M, D), BF16)],
        compiler_params=_params("parallel"),
    )(act, w, ple, x, g, b)


def _block_diag(w):
    L, G, gs, _ = w.shape
    per = LRU_BD // gs
    w5 = w.reshape(L, G // per, per, gs, gs)
    eye = jnp.eye(per, dtype=w.dtype)
    return jnp.einsum('lkaij,ab->lkaibj', w5, eye).reshape(L, G // per, LRU_BD, LRU_BD)


def kernel(x, p, w_in, lru_conv_w, lru_conv_b, lru_wr, lru_br, lru_wi, lru_bi, lru_lambda, gla_wg2, gla_bg2, gla_norm_g, gdn_conv_w, gdn_a_log, gdn_dt_bias, gdn_norm_g, w_branch, w_merge, b_merge, w_out, ln1_g, ln1_b, ffn_w_up, ffn_w_gate, ffn_conv_w, ffn_conv_b, ffn_w_down, ple_w_proj, ple_w_gate, ln2_g, ln2_b):
    B, S, D = x.shape
    L = w_in.shape[0]
    M = B * S

    o_lx, o_glr, o_gog, o_dq, o_da, o_dog, o_end = 0, 4096, 4112, 5136, 8208, 8224, 9248
    w_main = jnp.concatenate([w_in[:, :, o_dq:o_da], w_in[:, :, o_lx:o_glr], w_in[:, :, o_gog:o_dq],
                              w_in[:, :, o_dog:o_end]], axis=2).astype(BF16)
    w_small = jnp.concatenate([w_in[:, :, o_glr:o_gog], w_in[:, :, o_da:o_dog],
                               jnp.zeros((L, D, SMALL_COLS - GLA_RANK - 2 * GDN_HEADS), w_in.dtype)], axis=2).astype(BF16)
    nk = LRU_WIDTH // LRU_BD
    lru_wbd = jnp.concatenate([_block_diag(lru_wr), _block_diag(lru_wi)], axis=3).astype(BF16)
    lru_bri = jnp.concatenate([lru_br.reshape(L, nk, 1, LRU_BD), lru_bi.reshape(L, nk, 1, LRU_BD)], axis=3)
    wg2p = jnp.concatenate([gla_wg2, jnp.zeros((L, SMALL_COLS - GLA_RANK, gla_wg2.shape[2]), gla_wg2.dtype)],
                           axis=1).astype(BF16)
    lane_pad = lambda v: jnp.pad(v, ((0, 0), (DA_LANE, SMALL_COLS - DA_LANE - GDN_HEADS)))[:, None, :]
    alog_p, dtb_p = lane_pad(gdn_a_log), lane_pad(gdn_dt_bias)
    wm16, wb16, wo16 = w_merge.astype(BF16), w_branch.astype(BF16), w_out.astype(BF16)
    wu16, wgt16, wd16 = ffn_w_up.astype(BF16), ffn_w_gate.astype(BF16), ffn_w_down.astype(BF16)
    wp16, wpg16 = ple_w_proj.astype(BF16), ple_w_gate.astype(BF16)
    rows = lambda v: v[:, None, :]
    conv_w = jnp.concatenate([gdn_conv_w, lru_conv_w], axis=2)
    conv_b = rows(jnp.concatenate([jnp.zeros((L, gdn_conv_w.shape[2]), lru_conv_b.dtype), lru_conv_b], axis=1))
    pf = p.reshape(L, M, PLE_DIM)

    xf = x.reshape(M, D)
    xb = xf.astype(BF16)
    for i in range(L):
        proj = _proj(xb, w_main, conv_w, conv_b, i, S, 1024, 256).reshape(B, S, MAIN_COLS)
        small = _matmul(xb, w_small, i, F32, 2048, SMALL_COLS).reshape(B, S, SMALL_COLS)
        ya = _lru(proj, lru_wbd, lru_bri, rows(lru_lambda), i, 256)
        yb, yc = _gla_gdn(proj, small, wg2p, rows(gla_bg2), rows(gla_norm_g),
                          alog_p, dtb_p, rows(gdn_norm_g), i, 256)
        merged = _merge(xb, ya.reshape(M, -1), yb.reshape(M, -1), yc.reshape(M, -1),
                        wm16, wb16, b_merge[:, :, None, :], i, 512, 512)
        xf, xb = _out_ln(merged, wo16, xf, rows(ln1_g), rows(ln1_b), i, 512, 128)
        act = _ffn_up(xb, wu16, wgt16, ffn_conv_w, rows(ffn_conv_b), i, S, 2048, 512, 256)
        ple = _ple(pf, xb, wp16, wpg16, i, 1024, 1024)
        xf, xb = _down_ln(act, wd16, ple, xf, rows(ln2_g), rows(ln2_b), i, 256)
    return xf.reshape(B, S, D)
```

```python
import functools

import jax
import jax.numpy as jnp
from jax import lax
from jax.experimental import pallas as pl
from jax.experimental.pallas import tpu as pltpu

F32 = jnp.float32
BF16 = jnp.bfloat16

D_MODEL = 2048
DEPTH = 4
CHUNK = 64
PLE_DIM = 256
LRU_WIDTH = 1024
LRU_GROUPS = 16
LRU_CONV = 4
LRU_C = 8.0
GLA_HEADS = 4
GLA_DK = 128
GLA_DV = 256
GLA_RANK = 16
GLA_GATE_NORM = 16.0
GDN_HEADS = 8
GDN_DK = 128
GDN_DV = 128
GDN_CONV = 4
N_BRANCH = 3
D_FF = 5632
FFN_CONV = 3
LN_EPS = 1e-5
NORM_EPS = 1e-6
DEEPNORM_ALPHA = (2 * DEPTH) ** 0.25

MAIN_COLS = 9216
SMALL_COLS = 128
BLK_LRU_X, BLK_LRU_GATE, BLK_GQK, BLK_GV, BLK_GOG, BLK_DOG = 3, 4, 5, 6, 7, 8
GDN_TILES = 3
CONV_TILES = 4
assert LRU_CONV == GDN_CONV
PROJ_CONV = LRU_CONV
DA_LANE = 16
DB_LANE = 24
LRU_BD = 256
SUBLANES = 8
TAIL = SUBLANES

VMEM_LIMIT = 56 * 1024 * 1024


def _dot(a, b):
    return jnp.dot(a, b, preferred_element_type=F32)


def _dot_nt(a, b):
    return lax.dot_general(a, b, (((1,), (1,)), ((), ())), preferred_element_type=F32)


def _dot_tn(a, b):
    return lax.dot_general(a, b, (((0,), (0,)), ((), ())), preferred_element_type=F32)


def _split2(x):
    hi = x.astype(BF16)
    lo = (x - hi.astype(F32)).astype(BF16)
    return hi, lo


def _dot_hp(a, b):
    ah, al = _split2(a)
    bh, bl = _split2(b)
    return _dot(ah, bh) + (_dot(ah, bl) + _dot(al, bh))


def _split3(x):
    x1 = x.astype(BF16)
    r = x - x1.astype(F32)
    x2 = r.astype(BF16)
    x3 = (r - x2.astype(F32)).astype(BF16)
    return x1, x2, x3


def _softplus(x):
    return jnp.maximum(x, 0.0) + jnp.log1p(jnp.exp(-jnp.abs(x)))


def _sigmoid(x):
    return 1.0 / (1.0 + jnp.exp(-x))


def _silu(x):
    return x * _sigmoid(x)


def _layer_norm(y, g, b):
    mu = jnp.mean(y, axis=-1, keepdims=True)
    d = y - mu
    var = jnp.mean(d * d, axis=-1, keepdims=True)
    return d * lax.rsqrt(var + LN_EPS) * g + b


def _tile(dim, pref):
    t = min(dim, pref)
    assert dim % t == 0, (dim, pref)
    return t


def _params(*sem):
    return pltpu.CompilerParams(dimension_semantics=sem, vmem_limit_bytes=VMEM_LIMIT)


def _mm_body(x_ref, w_ref, o_ref):
    o_ref[...] = _dot(x_ref[...], w_ref[...]).astype(o_ref.dtype)


def _lspec(layer, shape, index_map=None):
    if index_map is None:
        index_map = lambda *g: (0,) * len(shape)
    return pl.BlockSpec((None,) + tuple(shape), lambda *g: (layer,) + tuple(index_map(*g)))


def _matmul(x, w, layer, out_dtype, tm, tn):
    M, K = x.shape
    N = w.shape[2]
    tm, tn = _tile(M, tm), _tile(N, tn)
    return pl.pallas_call(
        _mm_body,
        grid=(N // tn, M // tm),
        in_specs=[pl.BlockSpec((tm, K), lambda j, i: (i, 0)),
                  _lspec(layer, (K, tn), lambda j, i: (0, j))],
        out_specs=pl.BlockSpec((tm, tn), lambda j, i: (i, j)),
        out_shape=jax.ShapeDtypeStruct((M, N), out_dtype),
        compiler_params=_params("parallel", "parallel"),
    )(x, w)


def _proj_body(x_ref, w_ref, cw_ref, cb_ref, o_ref, g0, g1, *, slab):
    j = pl.program_id(0)
    tm = x_ref.shape[0]
    nslab = tm // slab

    def conv_path(act):
        cw = cw_ref[...]
        bufs = (g0, g1)
        prev_tail = jnp.zeros((TAIL, g0.shape[1]), F32)
        for s in range(nslab + 1):
            if s < nslab:
                gb = bufs[s % 2]
                g = _dot(x_ref[s * slab:(s + 1) * slab, :], w_ref[...])
                gb[0:TAIL, :] = prev_tail
                gb[TAIL:TAIL + slab, :] = g
                prev_tail = g[slab - TAIL:slab, :]
            if s >= 1:
                gb = bufs[(s - 1) % 2]
                y = cb_ref[...] + cw[PROJ_CONV - 1:PROJ_CONV, :] * gb[TAIL:TAIL + slab, :]
                for t in range(PROJ_CONV - 1):
                    y = y + cw[t:t + 1, :] * gb[pl.ds(TAIL - (PROJ_CONV - 1) + t, slab), :]
                o_ref[(s - 1) * slab:s * slab, :] = act(y)

    @pl.when(j < GDN_TILES)
    def _():
        conv_path(_silu)

    @pl.when((j >= GDN_TILES) & (j < CONV_TILES))
    def _():
        conv_path(lambda y: y)

    @pl.when(j >= CONV_TILES)
    def _():
        o_ref[...] = _dot(x_ref[...], w_ref[...])


def _proj(x, w, cw, cb, layer, seq, tn, slab):
    M, K = x.shape
    N = w.shape[2]
    assert N % tn == 0 and tn == 1024 and seq % slab == 0
    last = CONV_TILES - 1
    return pl.pallas_call(
        functools.partial(_proj_body, slab=slab),
        grid=(N // tn, M // seq),
        in_specs=[pl.BlockSpec((seq, K), lambda j, i: (i, 0)),
                  _lspec(layer, (K, tn), lambda j, i: (0, j)),
                  _lspec(layer, (PROJ_CONV, tn), lambda j, i: (0, jnp.minimum(j, last))),
                  _lspec(layer, (1, tn), lambda j, i: (0, jnp.minimum(j, last)))],
        out_specs=pl.BlockSpec((seq, tn), lambda j, i: (i, j)),
        out_shape=jax.ShapeDtypeStruct((M, N), F32),
        scratch_shapes=[pltpu.VMEM((slab + TAIL, tn), F32), pltpu.VMEM((slab + TAIL, tn), F32)],
        compiler_params=_params("parallel", "parallel"),
    )(x, w, cw, cb)


def _lru_body(x_ref, gate_ref, wbd_ref, bri_ref, lam_ref, o_ref, hcarry):
    tc = x_ref.shape[1]

    @pl.when(pl.program_id(1) == 0)
    def _():
        hcarry[...] = jnp.zeros_like(hcarry)

    xa = x_ref[0]
    lam = lam_ref[...]
    neg_c_sp = -LRU_C * _softplus(-lam)
    sub = lax.broadcasted_iota(jnp.int32, (tc // SUBLANES, SUBLANES, LRU_BD), 1)
    for k in range(LRU_WIDTH // LRU_BD):
        cs = slice(k * LRU_BD, (k + 1) * LRU_BD)
        xk = xa[:, cs]
        ri = _dot(xk.astype(BF16), wbd_ref[k]) + bri_ref[k]
        r = _sigmoid(ri[:, :LRU_BD])
        ig = _sigmoid(ri[:, LRU_BD:])
        log_a = r * neg_c_sp[:, cs]
        a = jnp.exp(log_a)
        u = jnp.sqrt(-jnp.tanh(log_a) * (a * a + 1.0)) * (ig * xk)
        a = a.reshape(tc // SUBLANES, SUBLANES, LRU_BD)
        u = u.reshape(tc // SUBLANES, SUBLANES, LRU_BD)
        d = 1
        while d < SUBLANES:
            keep = sub >= d
            a_s = jnp.where(keep, pltpu.roll(a, d, 1), 1.0)
            u_s = jnp.where(keep, pltpu.roll(u, d, 1), 0.0)
            u = a * u_s + u
            a = a * a_s
            d *= 2
        hc = hcarry[:, cs]
        hs = []
        for g in range(tc // SUBLANES):
            hg = u[g] + a[g] * hc
            hs.append(hg)
            hc = hg[SUBLANES - 1:SUBLANES, :]
        h = jnp.concatenate(hs, axis=0)
        hcarry[:, cs] = hc
        o_ref[0, :, cs] = (h * jax.nn.gelu(gate_ref[0, :, cs], approximate=True)).astype(o_ref.dtype)


def _lru(proj, wbd, bri, lam, layer, tc):
    B, S, _ = proj.shape
    tc = _tile(S, tc)
    W = LRU_WIDTH
    nk = W // LRU_BD
    full = lambda *shape: _lspec(layer, shape)
    return pl.pallas_call(
        _lru_body,
        grid=(B, S // tc),
        in_specs=[pl.BlockSpec((1, tc, W), lambda b, s: (b, s, BLK_LRU_X)),
                  pl.BlockSpec((1, tc, W), lambda b, s: (b, s, BLK_LRU_GATE)),
                  full(nk, LRU_BD, 2 * LRU_BD), full(nk, 1, 2 * LRU_BD), full(1, W)],
        out_specs=pl.BlockSpec((1, tc, W), lambda b, s: (b, s, 0)),
        out_shape=jax.ShapeDtypeStruct((B, S, W), BF16),
        scratch_shapes=[pltpu.VMEM((1, W), F32)],
        compiler_params=_params("parallel", "arbitrary"),
    )(proj, proj, wbd, bri, lam)


def _gla_body(qk_ref, v_ref, og_ref, sm_ref, wg2_ref, bg2_ref, ng_ref, o_ref, state):
    H, K, V, C = GLA_HEADS, GLA_DK, GLA_DV, CHUNK
    T = qk_ref.shape[1]
    chunks = range(T // C)
    heads = range(H)
    z = _dot(sm_ref[0].astype(BF16), wg2_ref[...]) + bg2_ref[...]
    fg = (jnp.minimum(z, 0.0) - jnp.log1p(jnp.exp(-jnp.abs(z)))) * (1.0 / GLA_GATE_NORM)
    tt = lax.broadcasted_iota(jnp.int32, (T, T), 0)
    ss = lax.broadcasted_iota(jnp.int32, (T, T), 1)
    tri = jnp.where((tt >= ss) & (tt // C == ss // C), 1.0, 0.0).astype(BF16)
    f1, f2, f3 = _split3(fg)
    bcum = _dot(tri, f1) + (_dot(tri, f2) + _dot(tri, f3))
    causal = lax.broadcasted_iota(jnp.int32, (C, C), 0) >= lax.broadcasted_iota(jnp.int32, (C, C), 1)
    ng = ng_ref[...]
    qk = qk_ref[0]
    q_all = qk[:, :H * K] * (K ** -0.5)
    k_all = qk[:, H * K:]
    q_dec_all = (q_all * jnp.exp(bcum)).astype(BF16)
    k_neg_all = (k_all * jnp.exp(-bcum)).astype(BF16)
    rows = [slice(c * C, (c + 1) * C) for c in chunks]
    bl = [bcum[(c + 1) * C - 1:(c + 1) * C, :] for c in chunks]
    k_dec = [(k_all[rows[c], :] * jnp.exp(bl[c] - bcum[rows[c], :])).astype(BF16) for c in chunks]
    ebl = [jnp.exp(bl[c]) for c in chunks]
    ks = [slice(h * K, (h + 1) * K) for h in heads]
    vs = [slice(h * V, (h + 1) * V) for h in heads]
    v = [[v_ref[0, rows[c], vs[h]].astype(BF16) for h in heads] for c in chunks]
    scores = [[jnp.where(causal, _dot_nt(q_dec_all[rows[c], ks[h]], k_neg_all[rows[c], ks[h]]), 0.0).astype(BF16)
               for h in heads] for c in chunks]
    intra = [[_dot(scores[c][h], v[c][h]) for h in heads] for c in chunks]
    def step(c, st):
        o = [intra[c][h] + _dot_nt(q_dec_all[rows[c], ks[h]], st[h].astype(BF16)) for h in heads]
        st = [st[h] * ebl[c][:, ks[h]] + _dot_tn(v[c][h], k_dec[c][:, ks[h]]) for h in heads]
        for h in heads:
            on = o[h] * lax.rsqrt(jnp.mean(o[h] * o[h], axis=-1, keepdims=True) + NORM_EPS) * ng
            o_ref[0, rows[c], vs[h]] = (on * _silu(og_ref[0, rows[c], vs[h]])).astype(o_ref.dtype)
        return st

    return [state[h] for h in heads], step


def _unit_lower_inverses(n_list, ti, si):
    C = n_list[0].shape[0]
    eye = jnp.where(ti == si, 1.0, 0.0)
    pair = ti // 2 == si // 2
    ts = [eye - jnp.where(pair, n, 0.0) for n in n_list]
    d = 2
    while d < C:
        m = (ti // (2 * d) == si // (2 * d)) & (ti % (2 * d) >= d) & (si % (2 * d) < d)
        cds = [jnp.where(m, n, 0.0).astype(BF16) for n in n_list]
        tbs = [t.astype(BF16) for t in ts]
        tcs = [_dot(tb, cd).astype(BF16) for tb, cd in zip(tbs, cds)]
        ts = [t - _dot(tc, tb) for t, tc, tb in zip(ts, tcs, tbs)]
        d *= 2
    return ts


def _gdn_body(qkv_ref, og_ref, sm_ref, alog_ref, dtb_ref, ng_ref, o_ref, state):
    H, K, V, C = GDN_HEADS, GDN_DK, GDN_DV, CHUNK
    T = qkv_ref.shape[1]
    chunks = range(T // C)
    heads = range(H)
    qkv = qkv_ref[0]
    kn = [qkv[:, (H + h) * K:(H + h + 1) * K] for h in heads]
    kn = [x * lax.rsqrt(jnp.sum(x * x, axis=-1, keepdims=True) + NORM_EPS) for x in kn]
    qn = [qkv[:, h * K:(h + 1) * K] for h in heads]
    qn = [x * (lax.rsqrt(jnp.sum(x * x, axis=-1, keepdims=True) + NORM_EPS) * (K ** -0.5)) for x in qn]
    sm = sm_ref[0]
    beta_all = _sigmoid(sm)
    g_all = -jnp.exp(alog_ref[...]) * _softplus(sm + dtb_ref[...])
    tt = lax.broadcasted_iota(jnp.int32, (T, T), 0)
    ss = lax.broadcasted_iota(jnp.int32, (T, T), 1)
    tri = jnp.where((tt >= ss) & (tt // C == ss // C), 1.0, 0.0).astype(BF16)
    g1, g2, g3 = _split3(g_all)
    gam = _dot(tri, g1) + (_dot(tri, g2) + _dot(tri, g3))
    sel = jnp.where(lax.broadcasted_iota(jnp.int32, (SUBLANES, SMALL_COLS), 1)
                    == lax.broadcasted_iota(jnp.int32, (SUBLANES, SMALL_COLS), 0) + DA_LANE, 1.0, 0.0).astype(BF16)
    m1, m2, m3 = _split3(gam)
    gam_rows = _dot_nt(sel, m1) + (_dot_nt(sel, m2) + _dot_nt(sel, m3))
    egam = jnp.exp(gam)
    ti = lax.broadcasted_iota(jnp.int32, (C, C), 0)
    si = lax.broadcasted_iota(jnp.int32, (C, C), 1)
    incl = ti >= si
    strict = ti > si
    ng = ng_ref[...]
    rows = [slice(c * C, (c + 1) * C) for c in chunks]
    gl_row = [gam[(c + 1) * C - 1:(c + 1) * C, :] for c in chunks]
    ekd = [jnp.exp(gl_row[c] - gam[rows[c], :]) for c in chunks]
    elast = [jnp.exp(gl_row[c]) for c in chunks]

    pairs = [(c, h) for c in chunks for h in heads]
    lane = lambda x, c, h, base: x[rows[c], base + h:base + h + 1]
    gcol = [lane(gam, c, h, DA_LANE) for c, h in pairs]
    bt = [lane(beta_all, c, h, DB_LANE) for c, h in pairs]
    eg = [lane(egam, c, h, DA_LANE) for c, h in pairs]
    decay = [jnp.where(incl, jnp.exp(jnp.where(incl, gcol[p] - gam_rows[h:h + 1, rows[c]], 0.0)), 0.0)
             for p, (c, h) in enumerate(pairs)]
    q = [qn[h][rows[c], :] for c, h in pairs]
    k = [kn[h][rows[c], :] for c, h in pairs]
    v = [qkv[rows[c], 2 * H * K + h * V:2 * H * K + (h + 1) * V] for c, h in pairs]
    n = range(len(pairs))
    kb = [k[p] * bt[p] for p in n]
    k16 = [x.astype(BF16) for x in k]
    kk = [_dot_nt(kb[p].astype(BF16), k16[p]) * decay[p] for p in n]
    qk = [(_dot_nt(q[p].astype(BF16), k16[p]) * decay[p]).astype(BF16) for p in n]
    t_inv = _unit_lower_inverses([jnp.where(strict, x, 0.0) for x in kk], ti, si)
    t16 = [t.astype(BF16) for t in t_inv]
    rhs = [jnp.concatenate([v[p] * bt[p], kb[p] * eg[p]], axis=1).astype(BF16) for p in n]
    sol = [_dot(t16[p], rhs[p]) for p in n]
    q_dec = [(q[p] * eg[p]).astype(BF16) for p in n]
    k_dec = [(k[p] * lane(ekd[c], 0, h, DA_LANE)).astype(BF16) for p, (c, h) in enumerate(pairs)]
    def step_a(c, st):
        ps = [c * H + h for h in heads]
        st16 = [s.astype(BF16) for s in st]
        v_new = [sol[ps[h]][:, :V] - _dot(sol[ps[h]][:, V:].astype(BF16), st16[h]) for h in heads]
        o_st = [_dot(q_dec[ps[h]], st16[h]) for h in heads]
        return v_new, o_st

    def step_b(c, st, part):
        v_new, o_st = part
        ps = [c * H + h for h in heads]
        vn16 = [x.astype(BF16) for x in v_new]
        o = [o_st[h] + _dot(qk[ps[h]], vn16[h]) for h in heads]
        st = [st[h] * elast[c][:, DA_LANE + h:DA_LANE + h + 1] + _dot_tn(k_dec[ps[h]], vn16[h]) for h in heads]
        for h in heads:
            on = o[h] * lax.rsqrt(jnp.mean(o[h] * o[h], axis=-1, keepdims=True) + NORM_EPS) * ng
            o_ref[0, rows[c], h * V:(h + 1) * V] = (on * _silu(og_ref[0, rows[c], h * V:(h + 1) * V])).astype(o_ref.dtype)
        return st

    return [state[h] for h in heads], step_a, step_b


def _gla_gdn_body(qk_ref, v_ref, gog_ref, sm_ref, wg2_ref, bg2_ref, gng_ref,
                  qkv_ref, dog_ref, alog_ref, dtb_ref, dng_ref,
                  yb_ref, yc_ref, gstate, dstate):
    @pl.when(pl.program_id(1) == 0)
    def _():
        gstate[...] = jnp.zeros_like(gstate)
        dstate[...] = jnp.zeros_like(dstate)

    d_st, d_step_a, d_step_b = _gdn_body(qkv_ref, dog_ref, sm_ref, alog_ref, dtb_ref, dng_ref, yc_ref, dstate)
    g_st, g_step = _gla_body(qk_ref, v_ref, gog_ref, sm_ref, wg2_ref, bg2_ref, gng_ref, yb_ref, gstate)
    for c in range(qk_ref.shape[1] // CHUNK):
        part = d_step_a(c, d_st)
        g_st = g_step(c, g_st)
        d_st = d_step_b(c, d_st, part)
    for h in range(GDN_HEADS):
        dstate[h] = d_st[h]
    for h in range(GLA_HEADS):
        gstate[h] = g_st[h]


def _gla_gdn(proj, small, wg2p, bg2, gng, alog_p, dtb_p, dng, layer, tt):
    B, S, _ = proj.shape
    C = _tile(S, tt)
    assert C % CHUNK == 0
    QKV = 2 * GDN_HEADS * GDN_DK + GDN_HEADS * GDN_DV
    W = GLA_HEADS * GLA_DV
    full = lambda *shape: _lspec(layer, shape)
    blk = lambda idx: pl.BlockSpec((1, C, W), lambda b, s: (b, s, idx))
    return pl.pallas_call(
        _gla_gdn_body,
        grid=(B, S // C),
        in_specs=[blk(BLK_GQK), blk(BLK_GV), blk(BLK_GOG),
                  pl.BlockSpec((1, C, SMALL_COLS), lambda b, s: (b, s, 0)),
                  full(SMALL_COLS, GLA_HEADS * GLA_DK), full(1, GLA_HEADS * GLA_DK), full(1, GLA_DV),
                  pl.BlockSpec((1, C, QKV), lambda b, s: (b, s, 0)), blk(BLK_DOG),
                  full(1, SMALL_COLS), full(1, SMALL_COLS), full(1, GDN_DV)],
        out_specs=[blk(0), blk(0)],
        out_shape=[jax.ShapeDtypeStruct((B, S, W), BF16), jax.ShapeDtypeStruct((B, S, W), BF16)],
        scratch_shapes=[pltpu.VMEM((GLA_HEADS, GLA_DV, GLA_DK), F32),
                        pltpu.VMEM((GDN_HEADS, GDN_DK, GDN_DV), F32)],
        compiler_params=_params("parallel", "arbitrary"),
    )(proj, proj, proj, small, wg2p, bg2, gng, proj, proj, alog_p, dtb_p, dng)


def _merge_body(h_ref, ya_ref, yb_ref, yc_ref, wm_ref, wb_ref, bm_ref, o_ref):
    hx = h_ref[...]
    acc = None
    for n, y_ref in enumerate((ya_ref, yb_ref, yc_ref)):
        gate = _sigmoid(_dot(hx, wm_ref[n]) + bm_ref[n])
        term = gate * _dot(y_ref[...], wb_ref[n])
        acc = term if acc is None else acc + term
    o_ref[...] = acc.astype(o_ref.dtype)


def _merge(hb, ya, yb, yc, wm, wb, bm, layer, tm, tn):
    M, D = hb.shape
    Wb = ya.shape[1]
    tm, tn = _tile(M, tm), _tile(D, tn)
    ybs = pl.BlockSpec((tm, Wb), lambda j, i: (i, 0))
    return pl.pallas_call(
        _merge_body,
        grid=(D // tn, M // tm),
        in_specs=[pl.BlockSpec((tm, D), lambda j, i: (i, 0)), ybs, ybs, ybs,
                  _lspec(layer, (N_BRANCH, D, tn), lambda j, i: (0, 0, j)),
                  _lspec(layer, (N_BRANCH, Wb, tn), lambda j, i: (0, 0, j)),
                  _lspec(layer, (N_BRANCH, 1, tn), lambda j, i: (0, 0, j))],
        out_specs=pl.BlockSpec((tm, tn), lambda j, i: (i, j)),
        out_shape=jax.ShapeDtypeStruct((M, D), BF16),
        compiler_params=_params("parallel", "parallel"),
    )(hb, ya, yb, yc, wm, wb, bm)


def _out_ln_body(m_ref, w_ref, x_ref, g_ref, b_ref, o_ref, ob_ref, y0, y1, *, slab):
    tm = m_ref.shape[0]
    ybufs = (y0, y1)
    for s in range(tm // slab + 1):
        if s < tm // slab:
            ybufs[s % 2][...] = _dot(m_ref[s * slab:(s + 1) * slab, :], w_ref[...])
        if s >= 1:
            rows = slice((s - 1) * slab, s * slab)
            y = DEEPNORM_ALPHA * x_ref[rows, :] + ybufs[(s - 1) % 2][...]
            xn = _layer_norm(y, g_ref[...], b_ref[...])
            o_ref[rows, :] = xn
            ob_ref[rows, :] = xn.astype(BF16)


def _out_ln(merged, w, x, g, b, layer, tm, slab):
    M, D = x.shape
    tm = _tile(M, tm)
    slab = _tile(tm, slab)
    row = pl.BlockSpec((tm, D), lambda i: (i, 0))
    vec = _lspec(layer, (1, D))
    return pl.pallas_call(
        functools.partial(_out_ln_body, slab=slab),
        grid=(M // tm,),
        in_specs=[row, pl.BlockSpec((None, D, D), lambda i: (layer, 0, 0), pipeline_mode=pl.Buffered(1)),
                  row, vec, vec],
        out_specs=[row, row],
        out_shape=[jax.ShapeDtypeStruct((M, D), F32), jax.ShapeDtypeStruct((M, D), BF16)],
        scratch_shapes=[pltpu.VMEM((slab, D), F32), pltpu.VMEM((slab, D), F32)],
        compiler_params=_params("parallel"),
    )(merged, w, x, g, b)


def _ffn_up_body(x_ref, wu32_ref, wg32_ref, cw_ref, cb_ref, o_ref, tail, g0, g1, u0, u1, wu_ref, wg_ref,
                 *, tiles_per_seq, slab):
    tm = x_ref.shape[0]

    @pl.when(pl.program_id(1) == 0)
    def _():
        wu_ref[...] = wu32_ref[...].astype(BF16)
        wg_ref[...] = wg32_ref[...].astype(BF16)

    @pl.when(pl.program_id(1) % tiles_per_seq == 0)
    def _():
        tail[...] = jnp.zeros_like(tail)

    cw = cw_ref[...]
    bufs = ((g0, u0), (g1, u1))
    prev_tail = tail[...]
    for s in range(tm // slab + 1):
        if s < tm // slab:
            gb, ub = bufs[s % 2]
            x = x_ref[s * slab:(s + 1) * slab, :]
            ub[...] = _dot(x, wu_ref[...])
            g = _dot(x, wg_ref[...])
            gb[0:TAIL, :] = prev_tail
            gb[TAIL:TAIL + slab, :] = g
            prev_tail = g[slab - TAIL:slab, :]
        if s >= 1:
            gb, ub = bufs[(s - 1) % 2]
            gt = cb_ref[...] + cw[2:3, :] * gb[TAIL:TAIL + slab, :]
            for j in range(FFN_CONV - 1):
                gt = gt + cw[j:j + 1, :] * gb[pl.ds(TAIL - (FFN_CONV - 1) + j, slab), :]
            o_ref[(s - 1) * slab:s * slab, :] = (_silu(gt) * ub[...]).astype(o_ref.dtype)
    tail[...] = prev_tail


def _ffn_up(xb, wu, wg, cw, cb, layer, seq, tm, tf, slab):
    M, D = xb.shape
    F = wu.shape[2]
    tm, tf = _tile(seq, tm), _tile(F, tf)
    slab = _tile(tm, slab)
    wspec = _lspec(layer, (D, tf), lambda j, i: (0, j))
    return pl.pallas_call(
        functools.partial(_ffn_up_body, tiles_per_seq=seq // tm, slab=slab),
        grid=(F // tf, M // tm),
        in_specs=[pl.BlockSpec((tm, D), lambda j, i: (i, 0)), wspec, wspec,
                  _lspec(layer, (FFN_CONV, tf), lambda j, i: (0, j)),
                  _lspec(layer, (1, tf), lambda j, i: (0, j))],
        out_specs=pl.BlockSpec((tm, tf), lambda j, i: (i, j)),
        out_shape=jax.ShapeDtypeStruct((M, F), BF16),
        scratch_shapes=[pltpu.VMEM((TAIL, tf), F32),
                        pltpu.VMEM((slab + TAIL, tf), F32), pltpu.VMEM((slab + TAIL, tf), F32),
                        pltpu.VMEM((slab, tf), F32), pltpu.VMEM((slab, tf), F32),
                        pltpu.VMEM((D, tf), BF16), pltpu.VMEM((D, tf), BF16)],
        compiler_params=_params("arbitrary", "arbitrary"),
    )(xb, wu, wg, cw, cb)


def _ple_body(p_ref, x_ref, wp_ref, wg_ref, o_ref):
    proj = _dot(p_ref[...].astype(BF16), wp_ref[...])
    o_ref[...] = proj * _sigmoid(_dot(x_ref[...], wg_ref[...]))


def _ple(p, xb, wp, wg, layer, tm, tn):
    M, D = xb.shape
    P = p.shape[2]
    tm, tn = _tile(M, tm), _tile(D, tn)
    return pl.pallas_call(
        _ple_body,
        grid=(D // tn, M // tm),
        in_specs=[_lspec(layer, (tm, P), lambda j, i: (i, 0)), pl.BlockSpec((tm, D), lambda j, i: (i, 0)),
                  _lspec(layer, (P, tn), lambda j, i: (0, j)), _lspec(layer, (D, tn), lambda j, i: (0, j))],
        out_specs=pl.BlockSpec((tm, tn), lambda j, i: (i, j)),
        out_shape=jax.ShapeDtypeStruct((M, D), F32),
        compiler_params=_params("parallel", "parallel"),
    )(p, xb, wp, wg)


def _down_ln_body(a_ref, w_ref, ple_ref, x_ref, g_ref, b_ref, o_ref, ob_ref, y0, y1, *, slab):
    tm = a_ref.shape[0]
    ybufs = (y0, y1)
    for s in range(tm // slab + 1):
        if s < tm // slab:
            ybufs[s % 2][...] = _dot(a_ref[s * slab:(s + 1) * slab, :], w_ref[...])
        if s >= 1:
            rows = slice((s - 1) * slab, s * slab)
            y = DEEPNORM_ALPHA * x_ref[rows, :] + ybufs[(s - 1) % 2][...] + ple_ref[rows, :]
            xn = _layer_norm(y, g_ref[...], b_ref[...])
            o_ref[rows, :] = xn
            ob_ref[rows, :] = xn.astype(BF16)


def _down_ln(act, w, ple, x, g, b, layer, tm, slab):
    M, D = x.shape
    F = act.shape[1]
    tm = _tile(M, tm)
    slab = _tile(tm, slab)
    row = pl.BlockSpec((tm, D), lambda i: (i, 0))
    vec = _lspec(layer, (1, D))
    return pl.pallas_call(
        functools.partial(_down_ln_body, slab=slab),
        grid=(M // tm,),
        in_specs=[pl.BlockSpec((tm, F), lambda i: (i, 0)),
                  pl.BlockSpec((None, F, D), lambda i: (layer, 0, 0), pipeline_mode=pl.Buffered(1)),
                  row, row, vec, vec],
        out_specs=[row, row],
        out_shape=[jax.ShapeDtypeStruct((M, D), F32), jax.ShapeDtypeStruct((M, D), BF16)],
        scratch_shapes=[pltpu.VMEM((slab, D), F32), pltpu.VMEM((slab, D), F32)],
        compiler_params=_params("parallel"),
    )(act, w, ple, x, g, b)


def _block_diag(w):
    L, G, gs, _ = w.shape
    per = LRU_BD // gs
    w5 = w.reshape(L, G // per, per, gs, gs)
    eye = jnp.eye(per, dtype=w.dtype)
    return jnp.einsum('lkaij,ab->lkaibj', w5, eye).reshape(L, G // per, LRU_BD, LRU_BD)


def kernel(x, p, w_in, lru_conv_w, lru_conv_b, lru_wr, lru_br, lru_wi, lru_bi, lru_lambda, gla_wg2, gla_bg2, gla_norm_g, gdn_conv_w, gdn_a_log, gdn_dt_bias, gdn_norm_g, w_branch, w_merge, b_merge, w_out, ln1_g, ln1_b, ffn_w_up, ffn_w_gate, ffn_conv_w, ffn_conv_b, ffn_w_down, ple_w_proj, ple_w_gate, ln2_g, ln2_b):
    B, S, D = x.shape
    L = w_in.shape[0]
    M = B * S

    o_lx, o_glr, o_gog, o_dq, o_da, o_dog, o_end = 0, 4096, 4112, 5136, 8208, 8224, 9248
    w_in16 = w_in.astype(BF16)
    w_main = jnp.concatenate([w_in16[:, :, o_dq:o_da], w_in16[:, :, o_lx:o_glr], w_in16[:, :, o_gog:o_dq],
                              w_in16[:, :, o_dog:o_end]], axis=2)
    w_small = jnp.concatenate([w_in16[:, :, o_glr:o_gog], w_in16[:, :, o_da:o_dog],
                               jnp.zeros((L, D, SMALL_COLS - GLA_RANK - 2 * GDN_HEADS), BF16)], axis=2)
    nk = LRU_WIDTH // LRU_BD
    lru_wbd = jnp.concatenate([_block_diag(lru_wr), _block_diag(lru_wi)], axis=3).astype(BF16)
    lru_bri = jnp.concatenate([lru_br.reshape(L, nk, 1, LRU_BD), lru_bi.reshape(L, nk, 1, LRU_BD)], axis=3)
    wg2p = jnp.concatenate([gla_wg2, jnp.zeros((L, SMALL_COLS - GLA_RANK, gla_wg2.shape[2]), gla_wg2.dtype)],
                           axis=1).astype(BF16)
    lane_pad = lambda v: jnp.pad(v, ((0, 0), (DA_LANE, SMALL_COLS - DA_LANE - GDN_HEADS)))[:, None, :]
    alog_p, dtb_p = lane_pad(gdn_a_log), lane_pad(gdn_dt_bias)
    wm16, wb16, wo16 = w_merge.astype(BF16), w_branch.astype(BF16), w_out.astype(BF16)
    wd16 = ffn_w_down.astype(BF16)
    wp16, wpg16 = ple_w_proj.astype(BF16), ple_w_gate.astype(BF16)
    rows = lambda v: v[:, None, :]
    conv_w = jnp.concatenate([gdn_conv_w, lru_conv_w], axis=2)
    conv_b = rows(jnp.concatenate([jnp.zeros((L, gdn_conv_w.shape[2]), lru_conv_b.dtype), lru_conv_b], axis=1))
    pf = p.reshape(L, M, PLE_DIM)

    xf = x.reshape(M, D)
    xb = xf.astype(BF16)
    for i in range(L):
        proj = _proj(xb, w_main, conv_w, conv_b, i, S, 1024, 256).reshape(B, S, MAIN_COLS)
        small = _matmul(xb, w_small, i, F32, 2048, SMALL_COLS).reshape(B, S, SMALL_COLS)
        ya = _lru(proj, lru_wbd, lru_bri, rows(lru_lambda), i, 256)
        yb, yc = _gla_gdn(proj, small, wg2p, rows(gla_bg2), rows(gla_norm_g),
                          alog_p, dtb_p, rows(gdn_norm_g), i, 256)
        merged = _merge(xb, ya.reshape(M, -1), yb.reshape(M, -1), yc.reshape(M, -1),
                        wm16, wb16, b_merge[:, :, None, :], i, 512, 512)
        xf, xb = _out_ln(merged, wo16, xf, rows(ln1_g), rows(ln1_b), i, 512, 128)
        act = _ffn_up(xb, ffn_w_up, ffn_w_gate, ffn_conv_w, rows(ffn_conv_b), i, S, 2048, 512, 256)
        ple = _ple(pf, xb, wp16, wpg16, i, 1024, 1024)
        xf, xb = _down_ln(act, wd16, ple, xf, rows(ln2_g), rows(ln2_b), i, 256, 128)
    return xf.reshape(B, S, D)
```

```python
import functools

import jax
import jax.numpy as jnp
from jax import lax
from jax.experimental import pallas as pl
from jax.experimental.pallas import tpu as pltpu

F32 = jnp.float32
BF16 = jnp.bfloat16

D_MODEL = 2048
DEPTH = 4
CHUNK = 64
PLE_DIM = 256
LRU_WIDTH = 1024
LRU_GROUPS = 16
LRU_CONV = 4
LRU_C = 8.0
GLA_HEADS = 4
GLA_DK = 128
GLA_DV = 256
GLA_RANK = 16
GLA_GATE_NORM = 16.0
GDN_HEADS = 8
GDN_DK = 128
GDN_DV = 128
GDN_CONV = 4
N_BRANCH = 3
D_FF = 5632
FFN_CONV = 3
LN_EPS = 1e-5
NORM_EPS = 1e-6
DEEPNORM_ALPHA = (2 * DEPTH) ** 0.25

MAIN_COLS = 9216
SMALL_COLS = 128
BLK_LRU_X, BLK_LRU_GATE, BLK_GQK, BLK_GV, BLK_GOG, BLK_DOG = 3, 4, 5, 6, 7, 8
GDN_TILES = 3
CONV_TILES = 4
assert LRU_CONV == GDN_CONV
PROJ_CONV = LRU_CONV
DA_LANE = 16
DB_LANE = 24
LRU_BD = 256
SUBLANES = 8
TAIL = SUBLANES

VMEM_LIMIT = 56 * 1024 * 1024


def _dot(a, b):
    return jnp.dot(a, b, preferred_element_type=F32)


def _dot_nt(a, b):
    return lax.dot_general(a, b, (((1,), (1,)), ((), ())), preferred_element_type=F32)


def _dot_tn(a, b):
    return lax.dot_general(a, b, (((0,), (0,)), ((), ())), preferred_element_type=F32)


def _split2(x):
    hi = x.astype(BF16)
    lo = (x - hi.astype(F32)).astype(BF16)
    return hi, lo


def _dot_hp(a, b):
    ah, al = _split2(a)
    bh, bl = _split2(b)
    return _dot(ah, bh) + (_dot(ah, bl) + _dot(al, bh))


def _split3(x):
    x1 = x.astype(BF16)
    r = x - x1.astype(F32)
    x2 = r.astype(BF16)
    x3 = (r - x2.astype(F32)).astype(BF16)
    return x1, x2, x3


def _softplus(x):
    return jnp.maximum(x, 0.0) + jnp.log1p(jnp.exp(-jnp.abs(x)))


def _sigmoid(x):
    return 1.0 / (1.0 + jnp.exp(-x))


def _silu(x):
    return x * _sigmoid(x)


def _layer_norm(y, g, b):
    mu = jnp.mean(y, axis=-1, keepdims=True)
    d = y - mu
    var = jnp.mean(d * d, axis=-1, keepdims=True)
    return d * lax.rsqrt(var + LN_EPS) * g + b


def _tile(dim, pref):
    t = min(dim, pref)
    assert dim % t == 0, (dim, pref)
    return t


def _params(*sem):
    return pltpu.CompilerParams(dimension_semantics=sem, vmem_limit_bytes=VMEM_LIMIT)


def _mm_body(x_ref, w_ref, o_ref):
    o_ref[...] = _dot(x_ref[...], w_ref[...]).astype(o_ref.dtype)


def _lspec(layer, shape, index_map=None):
    if index_map is None:
        index_map = lambda *g: (0,) * len(shape)
    return pl.BlockSpec((None,) + tuple(shape), lambda *g: (layer,) + tuple(index_map(*g)))


def _matmul(x, w, layer, out_dtype, tm, tn):
    M, K = x.shape
    N = w.shape[2]
    tm, tn = _tile(M, tm), _tile(N, tn)
    return pl.pallas_call(
        _mm_body,
        grid=(N // tn, M // tm),
        in_specs=[pl.BlockSpec((tm, K), lambda j, i: (i, 0)),
                  _lspec(layer, (K, tn), lambda j, i: (0, j))],
        out_specs=pl.BlockSpec((tm, tn), lambda j, i: (i, j)),
        out_shape=jax.ShapeDtypeStruct((M, N), out_dtype),
        compiler_params=_params("parallel", "parallel"),
    )(x, w)


def _proj_body(x_ref, w_ref, cw_ref, cb_ref, o_ref, g0, g1, *, slab):
    j = pl.program_id(0)
    tm = x_ref.shape[0]
    nslab = tm // slab

    def conv_path(act):
        cw = cw_ref[...]
        bufs = (g0, g1)
        prev_tail = jnp.zeros((TAIL, g0.shape[1]), F32)
        for s in range(nslab + 1):
            if s < nslab:
                gb = bufs[s % 2]
                g = _dot(x_ref[s * slab:(s + 1) * slab, :], w_ref[...])
                gb[0:TAIL, :] = prev_tail
                gb[TAIL:TAIL + slab, :] = g
                prev_tail = g[slab - TAIL:slab, :]
            if s >= 1:
                gb = bufs[(s - 1) % 2]
                y = cb_ref[...] + cw[PROJ_CONV - 1:PROJ_CONV, :] * gb[TAIL:TAIL + slab, :]
                for t in range(PROJ_CONV - 1):
                    y = y + cw[t:t + 1, :] * gb[pl.ds(TAIL - (PROJ_CONV - 1) + t, slab), :]
                o_ref[(s - 1) * slab:s * slab, :] = act(y)

    @pl.when(j < GDN_TILES)
    def _():
        conv_path(_silu)

    @pl.when((j >= GDN_TILES) & (j < CONV_TILES))
    def _():
        conv_path(lambda y: y)

    @pl.when(j >= CONV_TILES)
    def _():
        o_ref[...] = _dot(x_ref[...], w_ref[...])


def _proj(x, w, cw, cb, layer, seq, tn, slab):
    M, K = x.shape
    N = w.shape[2]
    assert N % tn == 0 and tn == 1024 and seq % slab == 0
    last = CONV_TILES - 1
    return pl.pallas_call(
        functools.partial(_proj_body, slab=slab),
        grid=(N // tn, M // seq),
        in_specs=[pl.BlockSpec((seq, K), lambda j, i: (i, 0)),
                  _lspec(layer, (K, tn), lambda j, i: (0, j)),
                  _lspec(layer, (PROJ_CONV, tn), lambda j, i: (0, jnp.minimum(j, last))),
                  _lspec(layer, (1, tn), lambda j, i: (0, jnp.minimum(j, last)))],
        out_specs=pl.BlockSpec((seq, tn), lambda j, i: (i, j)),
        out_shape=jax.ShapeDtypeStruct((M, N), F32),
        scratch_shapes=[pltpu.VMEM((slab + TAIL, tn), F32), pltpu.VMEM((slab + TAIL, tn), F32)],
        compiler_params=_params("parallel", "parallel"),
    )(x, w, cw, cb)


def _lru_body(x_ref, gate_ref, wbd_ref, bri_ref, lam_ref, o_ref, hcarry):
    tc = x_ref.shape[1]

    @pl.when(pl.program_id(1) == 0)
    def _():
        hcarry[...] = jnp.zeros_like(hcarry)

    xa = x_ref[0]
    lam = lam_ref[...]
    neg_c_sp = -LRU_C * _softplus(-lam)
    sub = lax.broadcasted_iota(jnp.int32, (tc // SUBLANES, SUBLANES, LRU_BD), 1)
    for k in range(LRU_WIDTH // LRU_BD):
        cs = slice(k * LRU_BD, (k + 1) * LRU_BD)
        xk = xa[:, cs]
        ri = _dot(xk.astype(BF16), wbd_ref[k]) + bri_ref[k]
        r = _sigmoid(ri[:, :LRU_BD])
        ig = _sigmoid(ri[:, LRU_BD:])
        log_a = r * neg_c_sp[:, cs]
        a = jnp.exp(log_a)
        u = jnp.sqrt(-jnp.tanh(log_a) * (a * a + 1.0)) * (ig * xk)
        a = a.reshape(tc // SUBLANES, SUBLANES, LRU_BD)
        u = u.reshape(tc // SUBLANES, SUBLANES, LRU_BD)
        d = 1
        while d < SUBLANES:
            keep = sub >= d
            a_s = jnp.where(keep, pltpu.roll(a, d, 1), 1.0)
            u_s = jnp.where(keep, pltpu.roll(u, d, 1), 0.0)
            u = a * u_s + u
            a = a * a_s
            d *= 2
        hc = hcarry[:, cs]
        hs = []
        for g in range(tc // SUBLANES):
            hg = u[g] + a[g] * hc
            hs.append(hg)
            hc = hg[SUBLANES - 1:SUBLANES, :]
        h = jnp.concatenate(hs, axis=0)
        hcarry[:, cs] = hc
        o_ref[0, :, cs] = (h * jax.nn.gelu(gate_ref[0, :, cs], approximate=True)).astype(o_ref.dtype)


def _lru(proj, wbd, bri, lam, layer, tc):
    B, S, _ = proj.shape
    tc = _tile(S, tc)
    W = LRU_WIDTH
    nk = W // LRU_BD
    full = lambda *shape: _lspec(layer, shape)
    return pl.pallas_call(
        _lru_body,
        grid=(B, S // tc),
        in_specs=[pl.BlockSpec((1, tc, W), lambda b, s: (b, s, BLK_LRU_X)),
                  pl.BlockSpec((1, tc, W), lambda b, s: (b, s, BLK_LRU_GATE)),
                  full(nk, LRU_BD, 2 * LRU_BD), full(nk, 1, 2 * LRU_BD), full(1, W)],
        out_specs=pl.BlockSpec((1, tc, W), lambda b, s: (b, s, 0)),
        out_shape=jax.ShapeDtypeStruct((B, S, W), BF16),
        scratch_shapes=[pltpu.VMEM((1, W), F32)],
        compiler_params=_params("parallel", "arbitrary"),
    )(proj, proj, wbd, bri, lam)


def _gla_body(qk_ref, v_ref, og_ref, sm_ref, wg2_ref, bg2_ref, ng_ref, o_ref, state):
    H, K, V, C = GLA_HEADS, GLA_DK, GLA_DV, CHUNK
    T = qk_ref.shape[1]
    chunks = range(T // C)
    heads = range(H)
    z = _dot(sm_ref[0].astype(BF16), wg2_ref[...]) + bg2_ref[...]
    fg = (jnp.minimum(z, 0.0) - jnp.log1p(jnp.exp(-jnp.abs(z)))) * (1.0 / GLA_GATE_NORM)
    tt = lax.broadcasted_iota(jnp.int32, (T, T), 0)
    ss = lax.broadcasted_iota(jnp.int32, (T, T), 1)
    tri = jnp.where((tt >= ss) & (tt // C == ss // C), 1.0, 0.0).astype(BF16)
    f1, f2, f3 = _split3(fg)
    bcum = _dot(tri, f1) + (_dot(tri, f2) + _dot(tri, f3))
    causal = lax.broadcasted_iota(jnp.int32, (C, C), 0) >= lax.broadcasted_iota(jnp.int32, (C, C), 1)
    ng = ng_ref[...]
    qk = qk_ref[0]
    q_all = qk[:, :H * K] * (K ** -0.5)
    k_all = qk[:, H * K:]
    q_dec_all = (q_all * jnp.exp(bcum)).astype(BF16)
    k_neg_all = (k_all * jnp.exp(-bcum)).astype(BF16)
    rows = [slice(c * C, (c + 1) * C) for c in chunks]
    bl = [bcum[(c + 1) * C - 1:(c + 1) * C, :] for c in chunks]
    k_dec = [(k_all[rows[c], :] * jnp.exp(bl[c] - bcum[rows[c], :])).astype(BF16) for c in chunks]
    ebl = [jnp.exp(bl[c]) for c in chunks]
    ks = [slice(h * K, (h + 1) * K) for h in heads]
    vs = [slice(h * V, (h + 1) * V) for h in heads]
    v = [[v_ref[0, rows[c], vs[h]].astype(BF16) for h in heads] for c in chunks]
    scores = [[jnp.where(causal, _dot_nt(q_dec_all[rows[c], ks[h]], k_neg_all[rows[c], ks[h]]), 0.0).astype(BF16)
               for h in heads] for c in chunks]
    intra = [[_dot(scores[c][h], v[c][h]) for h in heads] for c in chunks]
    def step(c, st):
        o = [intra[c][h] + _dot_nt(q_dec_all[rows[c], ks[h]], st[h].astype(BF16)) for h in heads]
        st = [st[h] * ebl[c][:, ks[h]] + _dot_tn(v[c][h], k_dec[c][:, ks[h]]) for h in heads]
        for h in heads:
            on = o[h] * lax.rsqrt(jnp.mean(o[h] * o[h], axis=-1, keepdims=True) + NORM_EPS) * ng
            o_ref[0, rows[c], vs[h]] = (on * _silu(og_ref[0, rows[c], vs[h]])).astype(o_ref.dtype)
        return st

    return [state[h] for h in heads], step


def _unit_lower_inverses(n_list, ti, si):
    C = n_list[0].shape[0]
    eye = jnp.where(ti == si, 1.0, 0.0)
    pair = ti // 2 == si // 2
    ts = [eye - jnp.where(pair, n, 0.0) for n in n_list]
    d = 2
    while d < C:
        m = (ti // (2 * d) == si // (2 * d)) & (ti % (2 * d) >= d) & (si % (2 * d) < d)
        cds = [jnp.where(m, n, 0.0).astype(BF16) for n in n_list]
        tbs = [t.astype(BF16) for t in ts]
        tcs = [_dot(tb, cd).astype(BF16) for tb, cd in zip(tbs, cds)]
        ts = [t - _dot(tc, tb) for t, tc, tb in zip(ts, tcs, tbs)]
        d *= 2
    return ts


def _gdn_body(qkv_ref, og_ref, sm_ref, alog_ref, dtb_ref, ng_ref, o_ref, state):
    H, K, V, C = GDN_HEADS, GDN_DK, GDN_DV, CHUNK
    T = qkv_ref.shape[1]
    chunks = range(T // C)
    heads = range(H)
    qkv = qkv_ref[0]
    kn = [qkv[:, (H + h) * K:(H + h + 1) * K] for h in heads]
    kn = [x * lax.rsqrt(jnp.sum(x * x, axis=-1, keepdims=True) + NORM_EPS) for x in kn]
    qn = [qkv[:, h * K:(h + 1) * K] for h in heads]
    qn = [x * (lax.rsqrt(jnp.sum(x * x, axis=-1, keepdims=True) + NORM_EPS) * (K ** -0.5)) for x in qn]
    sm = sm_ref[0]
    beta_all = _sigmoid(sm)
    g_all = -jnp.exp(alog_ref[...]) * _softplus(sm + dtb_ref[...])
    tt = lax.broadcasted_iota(jnp.int32, (T, T), 0)
    ss = lax.broadcasted_iota(jnp.int32, (T, T), 1)
    tri = jnp.where((tt >= ss) & (tt // C == ss // C), 1.0, 0.0).astype(BF16)
    g1, g2, g3 = _split3(g_all)
    gam = _dot(tri, g1) + (_dot(tri, g2) + _dot(tri, g3))
    sel = jnp.where(lax.broadcasted_iota(jnp.int32, (SUBLANES, SMALL_COLS), 1)
                    == lax.broadcasted_iota(jnp.int32, (SUBLANES, SMALL_COLS), 0) + DA_LANE, 1.0, 0.0).astype(BF16)
    m1, m2, m3 = _split3(gam)
    gam_rows = _dot_nt(sel, m1) + (_dot_nt(sel, m2) + _dot_nt(sel, m3))
    egam = jnp.exp(gam)
    ti = lax.broadcasted_iota(jnp.int32, (C, C), 0)
    si = lax.broadcasted_iota(jnp.int32, (C, C), 1)
    incl = ti >= si
    strict = ti > si
    ng = ng_ref[...]
    rows = [slice(c * C, (c + 1) * C) for c in chunks]
    gl_row = [gam[(c + 1) * C - 1:(c + 1) * C, :] for c in chunks]
    ekd = [jnp.exp(gl_row[c] - gam[rows[c], :]) for c in chunks]
    elast = [jnp.exp(gl_row[c]) for c in chunks]

    pairs = [(c, h) for c in chunks for h in heads]
    lane = lambda x, c, h, base: x[rows[c], base + h:base + h + 1]
    gcol = [lane(gam, c, h, DA_LANE) for c, h in pairs]
    bt = [lane(beta_all, c, h, DB_LANE) for c, h in pairs]
    eg = [lane(egam, c, h, DA_LANE) for c, h in pairs]
    decay = [jnp.where(incl, jnp.exp(jnp.where(incl, gcol[p] - gam_rows[h:h + 1, rows[c]], 0.0)), 0.0)
             for p, (c, h) in enumerate(pairs)]
    q = [qn[h][rows[c], :] for c, h in pairs]
    k = [kn[h][rows[c], :] for c, h in pairs]
    v = [qkv[rows[c], 2 * H * K + h * V:2 * H * K + (h + 1) * V] for c, h in pairs]
    n = range(len(pairs))
    kb = [k[p] * bt[p] for p in n]
    k16 = [x.astype(BF16) for x in k]
    kk = [_dot_nt(kb[p].astype(BF16), k16[p]) * decay[p] for p in n]
    qk = [(_dot_nt(q[p].astype(BF16), k16[p]) * decay[p]).astype(BF16) for p in n]
    t_inv = _unit_lower_inverses([jnp.where(strict, x, 0.0) for x in kk], ti, si)
    t16 = [t.astype(BF16) for t in t_inv]
    rhs = [jnp.concatenate([v[p] * bt[p], kb[p] * eg[p]], axis=1).astype(BF16) for p in n]
    sol = [_dot(t16[p], rhs[p]) for p in n]
    q_dec = [(q[p] * eg[p]).astype(BF16) for p in n]
    k_dec = [(k[p] * lane(ekd[c], 0, h, DA_LANE)).astype(BF16) for p, (c, h) in enumerate(pairs)]
    def step_a(c, st):
        ps = [c * H + h for h in heads]
        st16 = [s.astype(BF16) for s in st]
        v_new = [sol[ps[h]][:, :V] - _dot(sol[ps[h]][:, V:].astype(BF16), st16[h]) for h in heads]
        o_st = [_dot(q_dec[ps[h]], st16[h]) for h in heads]
        return v_new, o_st

    def step_b(c, st, part):
        v_new, o_st = part
        ps = [c * H + h for h in heads]
        vn16 = [x.astype(BF16) for x in v_new]
        o = [o_st[h] + _dot(qk[ps[h]], vn16[h]) for h in heads]
        st = [st[h] * elast[c][:, DA_LANE + h:DA_LANE + h + 1] + _dot_tn(k_dec[ps[h]], vn16[h]) for h in heads]
        for h in heads:
            on = o[h] * lax.rsqrt(jnp.mean(o[h] * o[h], axis=-1, keepdims=True) + NORM_EPS) * ng
            o_ref[0, rows[c], h * V:(h + 1) * V] = (on * _silu(og_ref[0, rows[c], h * V:(h + 1) * V])).astype(o_ref.dtype)
        return st

    return [state[h] for h in heads], step_a, step_b


def _gla_gdn_body(qk_ref, v_ref, gog_ref, sm_ref, wg2_ref, bg2_ref, gng_ref,
                  qkv_ref, dog_ref, alog_ref, dtb_ref, dng_ref,
                  yb_ref, yc_ref, gstate, dstate):
    @pl.when(pl.program_id(1) == 0)
    def _():
        gstate[...] = jnp.zeros_like(gstate)
        dstate[...] = jnp.zeros_like(dstate)

    d_st, d_step_a, d_step_b = _gdn_body(qkv_ref, dog_ref, sm_ref, alog_ref, dtb_ref, dng_ref, yc_ref, dstate)
    g_st, g_step = _gla_body(qk_ref, v_ref, gog_ref, sm_ref, wg2_ref, bg2_ref, gng_ref, yb_ref, gstate)
    for c in range(qk_ref.shape[1] // CHUNK):
        part = d_step_a(c, d_st)
        g_st = g_step(c, g_st)
        d_st = d_step_b(c, d_st, part)
    for h in range(GDN_HEADS):
        dstate[h] = d_st[h]
    for h in range(GLA_HEADS):
        gstate[h] = g_st[h]


def _gla_gdn(proj, small, wg2p, bg2, gng, alog_p, dtb_p, dng, layer, tt):
    B, S, _ = proj.shape
    C = _tile(S, tt)
    assert C % CHUNK == 0
    QKV = 2 * GDN_HEADS * GDN_DK + GDN_HEADS * GDN_DV
    W = GLA_HEADS * GLA_DV
    full = lambda *shape: _lspec(layer, shape)
    blk = lambda idx: pl.BlockSpec((1, C, W), lambda b, s: (b, s, idx))
    return pl.pallas_call(
        _gla_gdn_body,
        grid=(B, S // C),
        in_specs=[blk(BLK_GQK), blk(BLK_GV), blk(BLK_GOG),
                  pl.BlockSpec((1, C, SMALL_COLS), lambda b, s: (b, s, 0)),
                  full(SMALL_COLS, GLA_HEADS * GLA_DK), full(1, GLA_HEADS * GLA_DK), full(1, GLA_DV),
                  pl.BlockSpec((1, C, QKV), lambda b, s: (b, s, 0)), blk(BLK_DOG),
                  full(1, SMALL_COLS), full(1, SMALL_COLS), full(1, GDN_DV)],
        out_specs=[blk(0), blk(0)],
        out_shape=[jax.ShapeDtypeStruct((B, S, W), BF16), jax.ShapeDtypeStruct((B, S, W), BF16)],
        scratch_shapes=[pltpu.VMEM((GLA_HEADS, GLA_DV, GLA_DK), F32),
                        pltpu.VMEM((GDN_HEADS, GDN_DK, GDN_DV), F32)],
        compiler_params=_params("parallel", "arbitrary"),
    )(proj, proj, proj, small, wg2p, bg2, gng, proj, proj, alog_p, dtb_p, dng)


def _merge_body(h_ref, ya_ref, yb_ref, yc_ref, wm_ref, wb_ref, bm_ref, o_ref):
    hx = h_ref[...]
    acc = None
    for n, y_ref in enumerate((ya_ref, yb_ref, yc_ref)):
        gate = _sigmoid(_dot(hx, wm_ref[n]) + bm_ref[n])
        term = gate * _dot(y_ref[...], wb_ref[n])
        acc = term if acc is None else acc + term
    o_ref[...] = acc.astype(o_ref.dtype)


def _merge(hb, ya, yb, yc, wm, wb, bm, layer, tm, tn):
    M, D = hb.shape
    Wb = ya.shape[1]
    tm, tn = _tile(M, tm), _tile(D, tn)
    ybs = pl.BlockSpec((tm, Wb), lambda j, i: (i, 0))
    return pl.pallas_call(
        _merge_body,
        grid=(D // tn, M // tm),
        in_specs=[pl.BlockSpec((tm, D), lambda j, i: (i, 0)), ybs, ybs, ybs,
                  _lspec(layer, (N_BRANCH, D, tn), lambda j, i: (0, 0, j)),
                  _lspec(layer, (N_BRANCH, Wb, tn), lambda j, i: (0, 0, j)),
                  _lspec(layer, (N_BRANCH, 1, tn), lambda j, i: (0, 0, j))],
        out_specs=pl.BlockSpec((tm, tn), lambda j, i: (i, j)),
        out_shape=jax.ShapeDtypeStruct((M, D), BF16),
        compiler_params=_params("parallel", "parallel"),
    )(hb, ya, yb, yc, wm, wb, bm)


def _out_ln_body(m_ref, w_ref, x_ref, g_ref, b_ref, o_ref, ob_ref, y0, y1, *, slab):
    tm = m_ref.shape[0]
    ybufs = (y0, y1)
    for s in range(tm // slab + 1):
        if s < tm // slab:
            ybufs[s % 2][...] = _dot(m_ref[s * slab:(s + 1) * slab, :], w_ref[...])
        if s >= 1:
            rows = slice((s - 1) * slab, s * slab)
            y = DEEPNORM_ALPHA * x_ref[rows, :] + ybufs[(s - 1) % 2][...]
            xn = _layer_norm(y, g_ref[...], b_ref[...])
            o_ref[rows, :] = xn
            ob_ref[rows, :] = xn.astype(BF16)


def _out_ln(merged, w, x, g, b, layer, tm, slab):
    M, D = x.shape
    tm = _tile(M, tm)
    slab = _tile(tm, slab)
    row = pl.BlockSpec((tm, D), lambda i: (i, 0))
    vec = _lspec(layer, (1, D))
    return pl.pallas_call(
        functools.partial(_out_ln_body, slab=slab),
        grid=(M // tm,),
        in_specs=[row, pl.BlockSpec((None, D, D), lambda i: (layer, 0, 0), pipeline_mode=pl.Buffered(1)),
                  row, vec, vec],
        out_specs=[row, row],
        out_shape=[jax.ShapeDtypeStruct((M, D), F32), jax.ShapeDtypeStruct((M, D), BF16)],
        scratch_shapes=[pltpu.VMEM((slab, D), F32), pltpu.VMEM((slab, D), F32)],
        compiler_params=_params("parallel"),
    )(merged, w, x, g, b)


def _ffn_up_body(x_ref, wu32_ref, wg32_ref, cw_ref, cb_ref, o_ref, tail, g0, g1, u0, u1, wu_ref, wg_ref,
                 *, tiles_per_seq, slab):
    tm = x_ref.shape[0]

    @pl.when(pl.program_id(1) == 0)
    def _():
        wu_ref[...] = wu32_ref[...].astype(BF16)
        wg_ref[...] = wg32_ref[...].astype(BF16)

    @pl.when(pl.program_id(1) % tiles_per_seq == 0)
    def _():
        tail[...] = jnp.zeros_like(tail)

    cw = cw_ref[...]
    bufs = ((g0, u0), (g1, u1))
    prev_tail = tail[...]
    for s in range(tm // slab + 1):
        if s < tm // slab:
            gb, ub = bufs[s % 2]
            x = x_ref[s * slab:(s + 1) * slab, :]
            ub[...] = _dot(x, wu_ref[...])
            g = _dot(x, wg_ref[...])
            gb[0:TAIL, :] = prev_tail
            gb[TAIL:TAIL + slab, :] = g
            prev_tail = g[slab - TAIL:slab, :]
        if s >= 1:
            gb, ub = bufs[(s - 1) % 2]
            gt = cb_ref[...] + cw[2:3, :] * gb[TAIL:TAIL + slab, :]
            for j in range(FFN_CONV - 1):
                gt = gt + cw[j:j + 1, :] * gb[pl.ds(TAIL - (FFN_CONV - 1) + j, slab), :]
            o_ref[(s - 1) * slab:s * slab, :] = (_silu(gt) * ub[...]).astype(o_ref.dtype)
    tail[...] = prev_tail


def _ffn_up(xb, wu, wg, cw, cb, layer, seq, tm, tf, slab):
    M, D = xb.shape
    F = wu.shape[2]
    tm, tf = _tile(seq, tm), _tile(F, tf)
    slab = _tile(tm, slab)
    wspec = _lspec(layer, (D, tf), lambda j, i: (0, j))
    return pl.pallas_call(
        functools.partial(_ffn_up_body, tiles_per_seq=seq // tm, slab=slab),
        grid=(F // tf, M // tm),
        in_specs=[pl.BlockSpec((tm, D), lambda j, i: (i, 0)), wspec, wspec,
                  _lspec(layer, (FFN_CONV, tf), lambda j, i: (0, j)),
                  _lspec(layer, (1, tf), lambda j, i: (0, j))],
        out_specs=pl.BlockSpec((tm, tf), lambda j, i: (i, j)),
        out_shape=jax.ShapeDtypeStruct((M, F), BF16),
        scratch_shapes=[pltpu.VMEM((TAIL, tf), F32),
                        pltpu.VMEM((slab + TAIL, tf), F32), pltpu.VMEM((slab + TAIL, tf), F32),
                        pltpu.VMEM((slab, tf), F32), pltpu.VMEM((slab, tf), F32),
                        pltpu.VMEM((D, tf), BF16), pltpu.VMEM((D, tf), BF16)],
        compiler_params=_params("arbitrary", "arbitrary"),
    )(xb, wu, wg, cw, cb)


def _ple_body(p_ref, x_ref, wp_ref, wg_ref, o_ref):
    proj = _dot(p_ref[...].astype(BF16), wp_ref[...])
    o_ref[...] = proj * _sigmoid(_dot(x_ref[...], wg_ref[...]))


def _ple(p, xb, wp, wg, layer, tm, tn):
    M, D = xb.shape
    P = p.shape[2]
    tm, tn = _tile(M, tm), _tile(D, tn)
    return pl.pallas_call(
        _ple_body,
        grid=(D // tn, M // tm),
        in_specs=[_lspec(layer, (tm, P), lambda j, i: (i, 0)), pl.BlockSpec((tm, D), lambda j, i: (i, 0)),
                  _lspec(layer, (P, tn), lambda j, i: (0, j)), _lspec(layer, (D, tn), lambda j, i: (0, j))],
        out_specs=pl.BlockSpec((tm, tn), lambda j, i: (i, j)),
        out_shape=jax.ShapeDtypeStruct((M, D), F32),
        compiler_params=_params("parallel", "parallel"),
    )(p, xb, wp, wg)


def _down_ln_body(a_ref, w_ref, ple_ref, x_ref, g_ref, b_ref, o_ref, ob_ref, y0, y1, *, slab):
    tm = a_ref.shape[0]
    ybufs = (y0, y1)
    for s in range(tm // slab + 1):
        if s < tm // slab:
            ybufs[s % 2][...] = _dot(a_ref[s * slab:(s + 1) * slab, :], w_ref[...])
        if s >= 1:
            rows = slice((s - 1) * slab, s * slab)
            y = DEEPNORM_ALPHA * x_ref[rows, :] + ybufs[(s - 1) % 2][...] + ple_ref[rows, :]
            xn = _layer_norm(y, g_ref[...], b_ref[...])
            o_ref[rows, :] = xn
            ob_ref[rows, :] = xn.astype(BF16)


def _down_ln(act, w, ple, x, g, b, layer, tm, slab):
    M, D = x.shape
    F = act.shape[1]
    tm = _tile(M, tm)
    slab = _tile(tm, slab)
    row = pl.BlockSpec((tm, D), lambda i: (i, 0))
    vec = _lspec(layer, (1, D))
    return pl.pallas_call(
        functools.partial(_down_ln_body, slab=slab),
        grid=(M // tm,),
        in_specs=[pl.BlockSpec((tm, F), lambda i: (i, 0)),
                  pl.BlockSpec((None, F, D), lambda i: (layer, 0, 0), pipeline_mode=pl.Buffered(1)),
                  row, row, vec, vec],
        out_specs=[row, row],
        out_shape=[jax.ShapeDtypeStruct((M, D), F32), jax.ShapeDtypeStruct((M, D), BF16)],
        scratch_shapes=[pltpu.VMEM((slab, D), F32), pltpu.VMEM((slab, D), F32)],
        compiler_params=_params("parallel"),
    )(act, w, ple, x, g, b)


def _block_diag(w):
    L, G, gs, _ = w.shape
    per = LRU_BD // gs
    w5 = w.reshape(L, G // per, per, gs, gs)
    eye = jnp.eye(per, dtype=w.dtype)
    return jnp.einsum('lkaij,ab->lkaibj', w5, eye).reshape(L, G // per, LRU_BD, LRU_BD)


def kernel(x, p, w_in, lru_conv_w, lru_conv_b, lru_wr, lru_br, lru_wi, lru_bi, lru_lambda, gla_wg2, gla_bg2, gla_norm_g, gdn_conv_w, gdn_a_log, gdn_dt_bias, gdn_norm_g, w_branch, w_merge, b_merge, w_out, ln1_g, ln1_b, ffn_w_up, ffn_w_gate, ffn_conv_w, ffn_conv_b, ffn_w_down, ple_w_proj, ple_w_gate, ln2_g, ln2_b):
    B, S, D = x.shape
    L = w_in.shape[0]
    M = B * S

    o_lx, o_glr, o_gog, o_dq, o_da, o_dog, o_end = 0, 4096, 4112, 5136, 8208, 8224, 9248
    w_in16 = w_in.astype(BF16)
    w_main = jnp.concatenate([w_in16[:, :, o_dq:o_da], w_in16[:, :, o_lx:o_glr], w_in16[:, :, o_gog:o_dq],
                              w_in16[:, :, o_dog:o_end]], axis=2)
    w_small = jnp.concatenate([w_in16[:, :, o_glr:o_gog], w_in16[:, :, o_da:o_dog],
                               jnp.zeros((L, D, SMALL_COLS - GLA_RANK - 2 * GDN_HEADS), BF16)], axis=2)
    nk = LRU_WIDTH // LRU_BD
    lru_wbd = jnp.concatenate([_block_diag(lru_wr), _block_diag(lru_wi)], axis=3).astype(BF16)
    lru_bri = jnp.concatenate([lru_br.reshape(L, nk, 1, LRU_BD), lru_bi.reshape(L, nk, 1, LRU_BD)], axis=3)
    wg2p = jnp.concatenate([gla_wg2, jnp.zeros((L, SMALL_COLS - GLA_RANK, gla_wg2.shape[2]), gla_wg2.dtype)],
                           axis=1).astype(BF16)
    lane_pad = lambda v: jnp.pad(v, ((0, 0), (DA_LANE, SMALL_COLS - DA_LANE - GDN_HEADS)))[:, None, :]
    alog_p, dtb_p = lane_pad(gdn_a_log), lane_pad(gdn_dt_bias)
    wm16, wb16, wo16 = w_merge.astype(BF16), w_branch.astype(BF16), w_out.astype(BF16)
    wd16 = ffn_w_down.astype(BF16)
    wp16, wpg16 = ple_w_proj.astype(BF16), ple_w_gate.astype(BF16)
    rows = lambda v: v[:, None, :]
    conv_w = jnp.concatenate([gdn_conv_w, lru_conv_w], axis=2)
    conv_b = rows(jnp.concatenate([jnp.zeros((L, gdn_conv_w.shape[2]), lru_conv_b.dtype), lru_conv_b], axis=1))
    pf = p.reshape(L, M, PLE_DIM)

    xf = x.reshape(M, D)
    xb = xf.astype(BF16)
    for i in range(L):
        proj = _proj(xb, w_main, conv_w, conv_b, i, S, 1024, 256).reshape(B, S, MAIN_COLS)
        small = _matmul(xb, w_small, i, F32, 2048, SMALL_COLS).reshape(B, S, SMALL_COLS)
        ya = _lru(proj, lru_wbd, lru_bri, rows(lru_lambda), i, 256)
        yb, yc = _gla_gdn(proj, small, wg2p, rows(gla_bg2), rows(gla_norm_g),
                          alog_p, dtb_p, rows(gdn_norm_g), i, 256)
        merged = _merge(xb, ya.reshape(M, -1), yb.reshape(M, -1), yc.reshape(M, -1),
                        wm16, wb16, b_merge[:, :, None, :], i, 1024, 512)
        xf, xb = _out_ln(merged, wo16, xf, rows(ln1_g), rows(ln1_b), i, 512, 128)
        act = _ffn_up(xb, ffn_w_up, ffn_w_gate, ffn_conv_w, rows(ffn_conv_b), i, S, 2048, 512, 256)
        ple = _ple(pf, xb, wp16, wpg16, i, 1024, 1024)
        xf, xb = _down_ln(act, wd16, ple, xf, rows(ln2_g), rows(ln2_b), i, 256, 128)
    return xf.reshape(B, S, D)
```
